```python
import math
import jax
import jax.numpy as jnp
from jax import lax
import numpy as np

D_MODEL = 2048
BATCH = 8
SEQ = 4096
DEPTH = 2

DN_HEADS = 8
DN_HEAD_DIM = 128
HG_HEADS = 4
HG_EXPAND = 128
HG_HEAD_DIM = 128
RT_HEADS = 4
RT_HEAD_DIM = 128
DN_WIDTH = DN_HEADS * DN_HEAD_DIM
HG_KEY_WIDTH = HG_HEADS * HG_EXPAND
HG_WIDTH = HG_HEADS * HG_HEAD_DIM
RT_WIDTH = RT_HEADS * RT_HEAD_DIM
MIX_WIDTH = DN_WIDTH + HG_WIDTH + RT_WIDTH
CONV_WIDTH = 4
CHUNK = 64
D_FF = 5632
PLE_DIM = 256
ROPE_THETA = 10000.0
NORM_EPS = 1e-6
IN_SIZES = (3 * DN_WIDTH, DN_WIDTH, DN_HEADS, DN_HEADS,
            HG_KEY_WIDTH, HG_KEY_WIDTH, HG_WIDTH, HG_WIDTH,
            RT_WIDTH, RT_WIDTH, RT_WIDTH, RT_WIDTH)
IN_WIDTH = sum(IN_SIZES)
F32 = jnp.float32

kernel_name = "hybrid_deltanet_hgrn2_retention_macaron"


def _rms_norm(x, w):
    xf = x.astype(F32)
    y = xf * lax.rsqrt(jnp.mean(xf * xf, axis=-1, keepdims=True) + NORM_EPS)
    return (y * w.astype(F32)).astype(x.dtype)


def _swiglu(h, w_gate_up, w_down):
    g, u = jnp.split(h @ w_gate_up, 2, axis=-1)
    return (jax.nn.silu(g) * u) @ w_down


def _split_cols(x, sizes):
    return jnp.split(x, np.cumsum(sizes)[:-1].tolist(), axis=-1)


def _to_heads(x, n_heads):
    b, s, w = x.shape
    return x.reshape(b, s, n_heads, w // n_heads).transpose(0, 2, 1, 3).astype(F32)


def _l2norm(x):
    return x * lax.rsqrt(jnp.sum(x * x, axis=-1, keepdims=True) + NORM_EPS)


def _head_norm_gate(o, gain, gate):
    b, h, s, d = o.shape
    o = o.transpose(0, 2, 1, 3)
    o = o * lax.rsqrt(jnp.mean(o * o, axis=-1, keepdims=True) + NORM_EPS) * gain.astype(F32)
    return o.reshape(b, s, h * d) * jax.nn.silu(gate.astype(F32))


def _chunk(x):
    b, h, s, d = x.shape
    return x.reshape(b, h, s // CHUNK, CHUNK, d).transpose(2, 0, 1, 3, 4)


def _unchunk(x):
    n, b, h, c, d = x.shape
    return x.transpose(1, 2, 0, 3, 4).reshape(b, h, n * c, d)


def _causal_dwconv(x, w):
    return lax.conv_general_dilated(
        x, w[:, None, :].astype(x.dtype), window_strides=(1,),
        padding=[(w.shape[0] - 1, 0)], dimension_numbers=("NWC", "WIO", "NWC"),
        feature_group_count=x.shape[-1])


def _rotary(x):
    s, d = x.shape[-2], x.shape[-1]
    half = d // 2
    inv_freq = ROPE_THETA ** (-jnp.arange(half, dtype=F32) / half)
    ang = jnp.arange(s, dtype=F32)[:, None] * inv_freq[None, :]
    cos, sin = jnp.cos(ang), jnp.sin(ang)
    x1, x2 = x[..., :half], x[..., half:]
    return jnp.concatenate([x1 * cos - x2 * sin, x2 * cos + x1 * sin], axis=-1)


def _gated_delta_rule(q, k, v, beta, g):
    b, h, _, dk = q.shape
    dv = v.shape[-1]
    qc, kc, vc = _chunk(q), _chunk(k), _chunk(v)
    bc = _chunk(beta[..., None])[..., 0]
    gc = jnp.cumsum(_chunk(g[..., None])[..., 0], axis=-1)
    incl = jnp.tril(jnp.ones((CHUNK, CHUNK), dtype=bool))
    strict = jnp.tril(jnp.ones((CHUNK, CHUNK), dtype=bool), k=-1)
    decay = jnp.exp(jnp.where(incl, gc[..., :, None] - gc[..., None, :], -jnp.inf))
    kb = kc * bc[..., None]
    lower = jnp.where(strict, jnp.einsum("nbhid,nbhjd->nbhij", kb, kc) * decay, 0.0)
    eye = jnp.eye(CHUNK, dtype=F32)
    rhs = jnp.concatenate([vc * bc[..., None], kb * jnp.exp(gc)[..., None]], axis=-1)
    sol = lax.linalg.triangular_solve(eye + lower, rhs, left_side=True, lower=True,
                                      unit_diagonal=True)
    u, w = sol[..., :dv], sol[..., dv:]
    qk = jnp.einsum("nbhid,nbhjd->nbhij", qc, kc) * decay

    def step(state, inp):
        q_, k_, u_, w_, qk_, g_ = inp
        v_new = u_ - jnp.einsum("bhcd,bhdv->bhcv", w_, state)
        o = (jnp.einsum("bhcd,bhdv->bhcv", q_ * jnp.exp(g_)[..., None], state)
             + jnp.einsum("bhij,bhjv->bhiv", qk_, v_new))
        g_last = g_[..., -1:]
        state = (state * jnp.exp(g_last)[..., None]
                 + jnp.einsum("bhcd,bhcv->bhdv", k_ * jnp.exp(g_last - g_)[..., None], v_new))
        return state, o

    _, o = lax.scan(step, jnp.zeros((b, h, dk, dv), F32), (qc, kc, u, w, qk, gc))
    return _unchunk(o)


def _hgrn2(q, k, v, log_f):
    b, h, _, dk = q.shape
    dv = v.shape[-1]
    qc, kc, vc = _chunk(q), _chunk(k), _chunk(v)
    bc = jnp.cumsum(_chunk(log_f), axis=-2)
    incl = jnp.tril(jnp.ones((CHUNK, CHUNK), dtype=bool))[:, :, None]

    def step(state, inp):
        q_, k_, v_, b_ = inp
        dec = jnp.exp(jnp.where(incl, b_[:, :, :, None, :] - b_[:, :, None, :, :], -jnp.inf))
        att = jnp.einsum("bhid,bhijd,bhjd->bhij", q_, dec, k_)
        o = (jnp.einsum("bhcd,bhdv->bhcv", q_ * jnp.exp(b_), state)
             + jnp.einsum("bhij,bhjv->bhiv", att, v_))
        b_last = b_[:, :, -1:, :]
        state = (jnp.exp(b_last)[:, :, 0, :, None] * state
                 + jnp.einsum("bhcd,bhcv->bhdv", k_ * jnp.exp(b_last - b_), v_))
        return state, o

    _, o = lax.scan(step, jnp.zeros((b, h, dk, dv), F32), (qc, kc, vc, bc))
    return _unchunk(o)


def _retention(q, k, v, log_gamma):
    b, h, _, dk = q.shape
    dv = v.shape[-1]
    qc, kc, vc = _chunk(q), _chunk(k), _chunk(v)
    pos = jnp.arange(CHUNK, dtype=F32)
    lg = log_gamma[:, None]
    incl = jnp.tril(jnp.ones((CHUNK, CHUNK), dtype=bool))
    dmat = jnp.exp(jnp.where(incl, (pos[:, None] - pos[None, :]) * lg[:, :, None], -jnp.inf))
    intra = jnp.einsum("nbhij,nbhjv->nbhiv", jnp.einsum("nbhid,nbhjd->nbhij", qc, kc) * dmat, vc)
    q_dec = jnp.exp((pos + 1.0) * lg)
    k_dec = jnp.exp((CHUNK - 1.0 - pos) * lg)
    chunk_dec = jnp.exp(CHUNK * log_gamma)
    kv = jnp.einsum("nbhcd,hc,nbhcv->nbhdv", kc, k_dec, vc)

    def step(state, kv_):
        return chunk_dec[:, None, None] * state + kv_, state

    _, s_before = lax.scan(step, jnp.zeros((b, h, dk, dv), F32), kv)
    inter = jnp.einsum("nbhcd,hc,nbhdv->nbhcv", qc, q_dec, s_before)
    return _unchunk(intra + inter)


def _hgrn_lower_bound(table, layer):
    s = jax.nn.softmax(table.astype(F32), axis=0)
    return (jnp.cumsum(s, axis=0) - s[0])[layer]


def _hybrid_mixer(hn, w_in, dn_conv, dn_a_log, dn_dt_bias, dn_out_norm, hg_lb,
                  hg_out_norm, rt_out_norm, w_out):
    (dn_qkv, dn_gate, dn_b, dn_a, hg_q, hg_f, hg_i, hg_gate,
     rt_q, rt_k, rt_v, rt_gate) = _split_cols(hn @ w_in, IN_SIZES)
    dn_q, dn_k, dn_v = jnp.split(jax.nn.silu(_causal_dwconv(dn_qkv, dn_conv)), 3, axis=-1)
    q = _l2norm(_to_heads(dn_q, DN_HEADS)) * DN_HEAD_DIM ** -0.5
    k = _l2norm(_to_heads(dn_k, DN_HEADS))
    v = _to_heads(dn_v, DN_HEADS)
    beta = jax.nn.sigmoid(dn_b.astype(F32)).transpose(0, 2, 1)
    g = -(jnp.exp(dn_a_log.astype(F32))
          * jax.nn.softplus(dn_a.astype(F32) + dn_dt_bias.astype(F32))).transpose(0, 2, 1)
    o_dn = _head_norm_gate(_gated_delta_rule(q, k, v, beta, g), dn_out_norm, dn_gate)
    z = _to_heads(hg_f, HG_HEADS)
    lb = hg_lb.reshape(HG_HEADS, 1, HG_EXPAND)[None]
    log_f = jnp.log(lb + (1.0 - lb) * jax.nn.sigmoid(z))
    hk = (1.0 - lb) * jax.nn.sigmoid(-z)
    hq = jax.nn.silu(_to_heads(hg_q, HG_HEADS)) * HG_EXPAND ** -0.5
    hv = _to_heads(hg_i, HG_HEADS)
    o_hg = _head_norm_gate(_hgrn2(hq, hk, hv, log_f), hg_out_norm, hg_gate)
    log_gamma = jnp.log(1.0 - 2.0 ** (-5.0 - jnp.arange(RT_HEADS, dtype=F32)))
    rq = _rotary(_to_heads(rt_q, RT_HEADS)) * RT_HEAD_DIM ** -0.5
    rk = _rotary(_to_heads(rt_k, RT_HEADS))
    rv = _to_heads(rt_v, RT_HEADS)
    o_rt = _head_norm_gate(_retention(rq, rk, rv, log_gamma), rt_out_norm, rt_gate)
    mixed = jnp.concatenate([o_dn, o_hg, o_rt], axis=-1).astype(hn.dtype)
    return mixed @ w_out


def setup_inputs(seed: int = 0) -> dict:
    key = jax.random.key(seed)
    ks = jax.random.split(key, 24)

    def nrm(k, shape, fan_in):
        return jax.random.normal(k, shape, F32) * fan_in ** -0.5

    def gain(k, shape):
        return 1.0 + 0.02 * jax.random.normal(k, shape, F32)

    dt = jnp.exp(jax.random.uniform(ks[9], (DEPTH, DN_HEADS), F32,
                                    math.log(1e-3), math.log(1e-1)))
    return {
        "x": jax.random.normal(ks[0], (BATCH, SEQ, D_MODEL), F32),
        "p": jax.random.normal(ks[1], (DEPTH, BATCH, SEQ, PLE_DIM), F32),
        "ffn1_norm": gain(ks[2], (DEPTH, D_MODEL)),
        "ffn1_w_gate_up": nrm(ks[3], (DEPTH, D_MODEL, 2 * D_FF), D_MODEL),
        "ffn1_w_down": nrm(ks[4], (DEPTH, D_FF, D_MODEL), D_FF),
        "mix_norm": gain(ks[5], (DEPTH, D_MODEL)),
        "w_in": nrm(ks[6], (DEPTH, D_MODEL, IN_WIDTH), D_MODEL),
        "dn_conv": nrm(ks[7], (DEPTH, CONV_WIDTH, 3 * DN_WIDTH), CONV_WIDTH),
        "dn_a_log": jnp.log(jax.random.uniform(ks[8], (DEPTH, DN_HEADS), F32, 1.0, 16.0)),
        "dn_dt_bias": dt + jnp.log(-jnp.expm1(-dt)),
        "dn_out_norm": gain(ks[10], (DEPTH, DN_HEAD_DIM)),
        "hg_lower_bounds": 0.1 * jax.random.normal(ks[11], (DEPTH, HG_KEY_WIDTH), F32),
        "hg_out_norm": gain(ks[12], (DEPTH, HG_HEAD_DIM)),
        "rt_out_norm": gain(ks[13], (DEPTH, RT_HEAD_DIM)),
        "w_out": nrm(ks[14], (DEPTH, MIX_WIDTH, D_MODEL), MIX_WIDTH),
        "ffn2_norm": gain(ks[15], (DEPTH, D_MODEL)),
        "ffn2_w_gate_up": nrm(ks[16], (DEPTH, D_MODEL, 2 * D_FF), D_MODEL),
        "ffn2_w_down": nrm(ks[17], (DEPTH, D_FF, D_MODEL), D_FF),
        "ple_norm": gain(ks[18], (DEPTH, D_MODEL)),
        "ple_w_gate": nrm(ks[19], (DEPTH, D_MODEL, D_MODEL), D_MODEL),
        "ple_w_proj": nrm(ks[20], (DEPTH, PLE_DIM, D_MODEL), PLE_DIM),
        "final_norm": gain(ks[21], (D_MODEL,)),
    }


def reference(x, p, ffn1_norm, ffn1_w_gate_up, ffn1_w_down, mix_norm, w_in, dn_conv,
              dn_a_log, dn_dt_bias, dn_out_norm, hg_lower_bounds, hg_out_norm, rt_out_norm,
              w_out, ffn2_norm, ffn2_w_gate_up, ffn2_w_down, ple_norm, ple_w_gate,
              ple_w_proj, final_norm):
    h = x
    for i in range(DEPTH):
        h = h + 0.5 * _swiglu(_rms_norm(h, ffn1_norm[i]), ffn1_w_gate_up[i], ffn1_w_down[i])
        lb = _hgrn_lower_bound(hg_lower_bounds, i)
        h = h + _hybrid_mixer(_rms_norm(h, mix_norm[i]), w_in[i], dn_conv[i], dn_a_log[i],
                              dn_dt_bias[i], dn_out_norm[i], lb, hg_out_norm[i],
                              rt_out_norm[i], w_out[i])
        h = h + 0.5 * _swiglu(_rms_norm(h, ffn2_norm[i]), ffn2_w_gate_up[i], ffn2_w_down[i])
        gate = jax.nn.sigmoid(_rms_norm(h, ple_norm[i]) @ ple_w_gate[i])
        h = h + gate * (p[i] @ ple_w_proj[i])
    return _rms_norm(h, final_norm)
```

```python
import functools
import math

import jax
import jax.numpy as jnp
from jax import lax
from jax.experimental import pallas as pl
from jax.experimental.pallas import tpu as pltpu

F32 = jnp.float32
BF16 = jnp.bfloat16

D_MODEL = 2048
D_FF = 5632
DEPTH = 2
DN_HEADS = 8
HG_HEADS = 4
RT_HEADS = 4
HEAD_DIM = 128
DN_WIDTH = DN_HEADS * HEAD_DIM
HG_WIDTH = HG_HEADS * HEAD_DIM
RT_WIDTH = RT_HEADS * HEAD_DIM
CONV_WIDTH = 4
PLE_DIM = 256
ROPE_THETA = 10000.0
NORM_EPS = 1e-6

LANES = 128
SUBLANES = 8
VMEM_LIMIT = 56 * 1024 * 1024

PROJ_MAIN = 3 * DN_WIDTH + DN_WIDTH + 4 * HG_WIDTH + 4 * RT_WIDTH
PROJ_WIDTH = PROJ_MAIN + 256
SMALL_COL_BLOCK = PROJ_MAIN // LANES

CHUNK = 128
HG_LEVELS = (64, 32, 16, 8, 4, 2, 1)


def _dot(a, b):
    return jnp.dot(a, b, preferred_element_type=F32)


def _dot_nt(a, b):
    return lax.dot_general(a, b, (((1,), (1,)), ((), ())), preferred_element_type=F32)


def _dot_tn(a, b):
    return lax.dot_general(a, b, (((0,), (0,)), ((), ())), preferred_element_type=F32)


def _bf(x):
    return x.astype(BF16)


def _sigmoid(x):
    return 1.0 / (1.0 + jnp.exp(-x))


def _silu(x):
    return x * _sigmoid(x)


def _rms_norm(x, w):
    return x * lax.rsqrt(jnp.mean(x * x, axis=-1, keepdims=True) + NORM_EPS) * w


def _split_bf16(x):
    hi = x.astype(BF16)
    lo = (x - hi.astype(F32)).astype(BF16)
    return hi, lo


def _params(*sem):
    return pltpu.CompilerParams(dimension_semantics=sem, vmem_limit_bytes=VMEM_LIMIT)


def _ffn_kernel(h_ref, nw_ref, wg_ref, wu_ref, wd_ref, o_ref, hn_ref, acc_ref, *, nf):
    j = pl.program_id(1)

    @pl.when(j == 0)
    def _():
        hn_ref[...] = _bf(_rms_norm(h_ref[...], nw_ref[...]))

    hn = hn_ref[...]
    g = _dot(hn, wg_ref[...])
    u = _dot(hn, wu_ref[...])
    d = _dot(_bf(_silu(g) * u), wd_ref[...])

    @pl.when(j == 0)
    def _():
        acc_ref[...] = d

    @pl.when(j > 0)
    def _():
        acc_ref[...] += d

    @pl.when(j == nf - 1)
    def _():
        o_ref[...] = h_ref[...] + 0.5 * acc_ref[...]


def _ffn(h, nw, wgu, wd, *, tm=512, tf=512):
    t, d = h.shape
    tm = min(tm, t)
    nf = D_FF // tf
    return pl.pallas_call(
        functools.partial(_ffn_kernel, nf=nf),
        grid=(t // tm, nf),
        in_specs=[
            pl.BlockSpec((tm, d), lambda i, j: (i, 0)),
            pl.BlockSpec((1, d), lambda i, j: (0, 0)),
            pl.BlockSpec((d, tf), lambda i, j: (0, j)),
            pl.BlockSpec((d, tf), lambda i, j: (0, j + nf)),
            pl.BlockSpec((tf, d), lambda i, j: (j, 0)),
        ],
        out_specs=pl.BlockSpec((tm, d), lambda i, j: (i, 0)),
        out_shape=jax.ShapeDtypeStruct((t, d), F32),
        scratch_shapes=[pltpu.VMEM((tm, d), BF16), pltpu.VMEM((tm, d), F32)],
        compiler_params=_params("parallel", "arbitrary"),
        name="ffn",
    )(h, nw, wgu, wgu, wd)


def _inproj_kernel(h_ref, nw_ref, w_ref, o_ref, hn_ref):
    @pl.when(pl.program_id(1) == 0)
    def _():
        hn_ref[...] = _bf(_rms_norm(h_ref[...], nw_ref[...]))

    o_ref[...] = _dot(hn_ref[...], w_ref[...])


def _inproj(h, nw, w, *, tm=512, tn=768):
    t, d = h.shape
    tm = min(tm, t)
    n = w.shape[1]
    return pl.pallas_call(
        _inproj_kernel,
        grid=(t // tm, n // tn),
        in_specs=[
            pl.BlockSpec((tm, d), lambda i, j: (i, 0)),
            pl.BlockSpec((1, d), lambda i, j: (0, 0)),
            pl.BlockSpec((d, tn), lambda i, j: (0, j)),
        ],
        out_specs=pl.BlockSpec((tm, tn), lambda i, j: (i, j)),
        out_shape=jax.ShapeDtypeStruct((t, n), F32),
        scratch_shapes=[pltpu.VMEM((tm, d), BF16)],
        compiler_params=_params("parallel", "arbitrary"),
        name="inproj",
    )(h, nw, w)


def _chunk_tril_bf16(blk):
    r = lax.broadcasted_iota(jnp.int32, (blk, blk), 0)
    c = lax.broadcasted_iota(jnp.int32, (blk, blk), 1)
    return jnp.where((r // CHUNK == c // CHUNK) & (c <= r), 1.0, 0.0).astype(BF16)


def _head_norm_gate(o, gain, gate):
    o = o * lax.rsqrt(jnp.mean(o * o, axis=-1, keepdims=True) + NORM_EPS) * gain
    return _bf(o * _silu(gate))


def _dn_kernel(qkv_ref, gate_ref, small_ref, conv_ref, alog_ref, dtb_ref, gain_ref, o_ref,
               xs_ref, state_ref, *, blk):
    s = pl.program_id(1)
    width = 3 * DN_WIDTH

    @pl.when(s == 0)
    def _():
        xs_ref[0:SUBLANES, :] = jnp.zeros((SUBLANES, width), F32)
        state_ref[...] = jnp.zeros_like(state_ref)

    xs_ref[SUBLANES:SUBLANES + blk, :] = qkv_ref[...]

    small = small_ref[...]
    beta_all = _sigmoid(small)
    x = small + dtb_ref[...]
    softplus = jnp.maximum(x, 0.0) + jnp.log(1.0 + jnp.exp(-jnp.abs(x)))
    g_all = -(jnp.exp(alog_ref[...]) * softplus)
    tril = _chunk_tril_bf16(blk)
    g_hi, g_lo = _split_bf16(g_all)
    gc_all = _dot(tril, g_hi) + _dot(tril, g_lo)
    gc_t = gc_all.T

    row = lax.broadcasted_iota(jnp.int32, (CHUNK, CHUNK), 0)
    col = lax.broadcasted_iota(jnp.int32, (CHUNK, CHUNK), 1)
    incl = col <= row
    strict = col < row
    eye = jnp.where(row == col, 1.0, 0.0).astype(F32)

    def conv_silu(c0):
        acc = None
        for j in range(CONV_WIDTH):
            start = SUBLANES - (CONV_WIDTH - 1) + j
            term = xs_ref[start:start + blk, c0:c0 + HEAD_DIM] * conv_ref[j:j + 1, c0:c0 + HEAD_DIM]
            acc = term if acc is None else acc + term
        return _silu(acc)

    for h in range(DN_HEADS):
        q = conv_silu(h * HEAD_DIM)
        k = conv_silu(DN_WIDTH + h * HEAD_DIM)
        v = conv_silu(2 * DN_WIDTH + h * HEAD_DIM)
        q = q * lax.rsqrt(jnp.sum(q * q, axis=-1, keepdims=True) + NORM_EPS) * (HEAD_DIM ** -0.5)
        k = k * lax.rsqrt(jnp.sum(k * k, axis=-1, keepdims=True) + NORM_EPS)
        beta = jnp.broadcast_to(beta_all[:, h:h + 1], (blk, HEAD_DIM))
        gc = jnp.broadcast_to(gc_all[:, DN_HEADS + h:DN_HEADS + h + 1], (blk, HEAD_DIM))
        gc_row = gc_t[DN_HEADS + h:DN_HEADS + h + 1, :]
        eg = jnp.exp(gc)
        kb = k * beta
        vb = v * beta
        kbeg = kb * eg
        qeg = q * eg
        state = state_ref[h]
        outs = []
        for c in range(blk // CHUNK):
            sl = slice(c * CHUNK, (c + 1) * CHUNK)
            gcc = gc[sl]
            decay = jnp.exp(jnp.where(incl, gcc - gc_row[:, sl], -jnp.inf))
            a = _dot_nt(_bf(jnp.concatenate([kb[sl], q[sl]], axis=0)), _bf(k[sl]))
            neg_l = -jnp.where(strict, a[:CHUNK] * decay, 0.0)
            qk = a[CHUNK:] * decay
            p = eye + neg_l
            m = _dot(_bf(neg_l), _bf(neg_l))
            for _ in range(int(math.log2(CHUNK)) - 2):
                mb = _bf(m)
                x2 = _dot(_bf(jnp.concatenate([m, p], axis=0)), mb)
                m = x2[:CHUNK]
                p = p + x2[CHUNK:]
            p = p + _dot(_bf(p), _bf(m))
            sol = _dot(_bf(p), _bf(jnp.concatenate([vb[sl], kbeg[sl]], axis=1)))
            u = sol[:, :HEAD_DIM]
            w = sol[:, HEAD_DIM:]
            ws = _dot(_bf(jnp.concatenate([w, qeg[sl]], axis=0)), _bf(state))
            v_new = u - ws[:CHUNK]
            outs.append(ws[CHUNK:] + _dot(_bf(qk), _bf(v_new)))
            g_last = gcc[CHUNK - 1:CHUNK, :]
            k_dec = k[sl] * jnp.exp(g_last - gcc)
            state = state * jnp.exp(g_last) + _dot_tn(_bf(k_dec), _bf(v_new))
        state_ref[h] = state
        o = jnp.concatenate(outs, axis=0) if len(outs) > 1 else outs[0]
        cols = slice(h * HEAD_DIM, (h + 1) * HEAD_DIM)
        o_ref[:, cols] = _head_norm_gate(o, gain_ref[...], gate_ref[:, cols])

    xs_ref[0:SUBLANES, :] = xs_ref[blk:blk + SUBLANES, :]


def _deltanet(proj, conv_w, alog_row, dtb_row, gain, *, batch, seq, blk=256):
    blk = min(blk, seq)
    nblk = seq // blk
    t = batch * seq
    row = lambda b, s: (b * nblk + s)
    return pl.pallas_call(
        functools.partial(_dn_kernel, blk=blk),
        grid=(batch, nblk),
        in_specs=[
            pl.BlockSpec((blk, 3 * DN_WIDTH), lambda b, s: (row(b, s), 0)),
            pl.BlockSpec((blk, DN_WIDTH), lambda b, s: (row(b, s), 3)),
            pl.BlockSpec((blk, LANES), lambda b, s: (row(b, s), SMALL_COL_BLOCK)),
            pl.BlockSpec((CONV_WIDTH, 3 * DN_WIDTH), lambda b, s: (0, 0)),
            pl.BlockSpec((1, LANES), lambda b, s: (0, 0)),
            pl.BlockSpec((1, LANES), lambda b, s: (0, 0)),
            pl.BlockSpec((1, HEAD_DIM), lambda b, s: (0, 0)),
        ],
        out_specs=pl.BlockSpec((blk, DN_WIDTH), lambda b, s: (row(b, s), 0)),
        out_shape=jax.ShapeDtypeStruct((t, DN_WIDTH), BF16),
        scratch_shapes=[pltpu.VMEM((blk + SUBLANES, 3 * DN_WIDTH), F32),
                        pltpu.VMEM((DN_HEADS, HEAD_DIM, HEAD_DIM), F32)],
        compiler_params=_params("parallel", "arbitrary"),
        name="deltanet",
    )(proj, proj, proj, conv_w, alog_row, dtb_row, gain)


def _hg_level_matrix():
    r = lax.broadcasted_iota(jnp.int32, (CHUNK, CHUNK), 0)
    c = lax.broadcasted_iota(jnp.int32, (CHUNK, CHUNK), 1)
    mats = [c <= r]
    for m in HG_LEVELS:
        mats.append(c <= (r // (2 * m)) * (2 * m) + m - 1)
    return jnp.concatenate([jnp.where(x, 1.0, 0.0).astype(BF16) for x in mats], axis=0)


def _hg_kernel(x_ref, lbt_ref, gain_ref, o_ref, state_ref, *, blk, layer):
    s = pl.program_id(1)

    @pl.when(s == 0)
    def _():
        state_ref[...] = jnp.zeros_like(state_ref)

    tbl = lbt_ref[...]
    e = jnp.exp(tbl - jnp.max(tbl, axis=0, keepdims=True))
    sm = e / jnp.sum(e, axis=0, keepdims=True)
    lb = jnp.zeros((1, HG_WIDTH), F32)
    for l in range(1, layer + 1):
        lb = lb + sm[l:l + 1, :]

    level_mat = _hg_level_matrix()
    row = lax.broadcasted_iota(jnp.int32, (CHUNK, CHUNK), 0)
    col = lax.broadcasted_iota(jnp.int32, (CHUNK, CHUNK), 1)
    rowv = lax.broadcasted_iota(jnp.int32, (CHUNK, HEAD_DIM), 0)

    for c in range(blk // CHUNK):
        sl = slice(c * CHUNK, (c + 1) * CHUNK)
        z = x_ref[sl, HG_WIDTH:2 * HG_WIDTH]
        log_f = jnp.log(lb + (1.0 - lb) * _sigmoid(z))
        hk_all = (1.0 - lb) * _sigmoid(-z)
        lf_hi, lf_lo = _split_bf16(log_f)
        y = _dot(level_mat, jnp.concatenate([lf_hi, lf_lo], axis=1))
        cums = y[:, :HG_WIDTH] + y[:, HG_WIDTH:]
        for h in range(HG_HEADS):
            cols = slice(h * HEAD_DIM, (h + 1) * HEAD_DIM)
            q = _silu(x_ref[sl, cols]) * (HEAD_DIM ** -0.5)
            k = hk_all[:, cols]
            v = x_ref[sl, 2 * HG_WIDTH + h * HEAD_DIM:2 * HG_WIDTH + (h + 1) * HEAD_DIM]
            b = cums[0:CHUNK, cols]
            att = jnp.zeros((CHUNK, CHUNK), F32)
            for li, m in enumerate(HG_LEVELS):
                ref = cums[(li + 1) * CHUNK:(li + 2) * CHUNK, cols]
                lower = (rowv // m) % 2 == 1
                ex = jnp.exp(jnp.where(lower, b - ref, ref - b))
                qt = jnp.where(lower, q * ex, 0.0)
                kt = jnp.where(lower, 0.0, k * ex)
                same_group = (row // (2 * m)) == (col // (2 * m))
                att = att + jnp.where(same_group, _dot_nt(_bf(qt), _bf(kt)), 0.0)
            diag = jnp.sum(q * k, axis=-1, keepdims=True)
            state_t = state_ref[h]
            o = (_dot(_bf(att), _bf(v)) + diag * v
                 + _dot_nt(_bf(q * jnp.exp(b)), _bf(state_t)))
            b_last = b[CHUNK - 1:CHUNK, :]
            k_dec = k * jnp.exp(b_last - b)
            state_ref[h] = state_t * jnp.exp(b_last) + _dot_tn(_bf(v), _bf(k_dec))
            gate = x_ref[sl, 3 * HG_WIDTH + h * HEAD_DIM:3 * HG_WIDTH + (h + 1) * HEAD_DIM]
            o_ref[sl, cols] = _head_norm_gate(o, gain_ref[...], gate)


def _hgrn2(proj, lb_table, gain, *, layer, batch, seq, blk=256):
    blk = min(blk, seq)
    nblk = seq // blk
    t = batch * seq
    return pl.pallas_call(
        functools.partial(_hg_kernel, blk=blk, layer=layer),
        grid=(batch, nblk),
        in_specs=[
            pl.BlockSpec((blk, 4 * HG_WIDTH), lambda b, s: (b * nblk + s, 2)),
            pl.BlockSpec((DEPTH, HG_WIDTH), lambda b, s: (0, 0)),
            pl.BlockSpec((1, HEAD_DIM), lambda b, s: (0, 0)),
        ],
        out_specs=pl.BlockSpec((blk, HG_WIDTH), lambda b, s: (b * nblk + s, 0)),
        out_shape=jax.ShapeDtypeStruct((t, HG_WIDTH), BF16),
        scratch_shapes=[pltpu.VMEM((HG_HEADS, HEAD_DIM, HEAD_DIM), F32)],
        compiler_params=_params("parallel", "arbitrary"),
        name="hgrn2",
    )(proj, lb_table, gain)


def _rope_kernel(cos_ref, sin_ref):
    shape = cos_ref.shape
    half = HEAD_DIM // 2
    pos = lax.broadcasted_iota(jnp.int32, shape, 0).astype(F32)
    lane = lax.broadcasted_iota(jnp.int32, shape, 1)
    inv_freq = jnp.exp((lane % half).astype(F32) * (-math.log(ROPE_THETA) / half))
    ang = pos * inv_freq
    cos_ref[...] = jnp.cos(ang)
    sin = jnp.sin(ang)
    sin_ref[...] = jnp.where(lane < half, -sin, sin)


def _rope_tables(seq):
    shape = jax.ShapeDtypeStruct((seq, HEAD_DIM), F32)
    return pl.pallas_call(_rope_kernel, out_shape=(shape, shape), name="rope_tables")()


def _rt_kernel(x_ref, cos_ref, sin_ref, gain_ref, o_ref, state_ref, *, blk):
    s = pl.program_id(1)

    @pl.when(s == 0)
    def _():
        state_ref[...] = jnp.zeros_like(state_ref)

    cos = cos_ref[...]
    sin = sin_ref[...]
    row = lax.broadcasted_iota(jnp.int32, (blk, blk), 0)
    col = lax.broadcasted_iota(jnp.int32, (blk, blk), 1)
    dist = (row - col).astype(F32)
    pos = lax.broadcasted_iota(jnp.int32, (blk, HEAD_DIM), 0).astype(F32)

    def rotary(x):
        return x * cos + pltpu.roll(x, HEAD_DIM // 2, axis=1) * sin

    for h in range(RT_HEADS):
        log_gamma = math.log(1.0 - 2.0 ** (-5.0 - h))
        cols = slice(h * HEAD_DIM, (h + 1) * HEAD_DIM)
        q = rotary(x_ref[:, cols]) * (HEAD_DIM ** -0.5)
        k = rotary(x_ref[:, RT_WIDTH + h * HEAD_DIM:RT_WIDTH + (h + 1) * HEAD_DIM])
        v = x_ref[:, 2 * RT_WIDTH + h * HEAD_DIM:2 * RT_WIDTH + (h + 1) * HEAD_DIM]
        vb = _bf(v)
        dmat = jnp.exp(jnp.where(col <= row, dist * log_gamma, -jnp.inf))
        att = _dot_nt(_bf(q), _bf(k)) * dmat
        state = state_ref[h]
        o = _dot(_bf(att), vb) + _dot(_bf(q * jnp.exp((pos + 1.0) * log_gamma)), _bf(state))
        k_dec = k * jnp.exp((blk - 1.0 - pos) * log_gamma)
        state_ref[h] = state * math.exp(blk * log_gamma) + _dot_tn(_bf(k_dec), vb)
        gate = x_ref[:, 3 * RT_WIDTH + h * HEAD_DIM:3 * RT_WIDTH + (h + 1) * HEAD_DIM]
        o_ref[:, cols] = _head_norm_gate(o, gain_ref[...], gate)


def _retention(proj, cos_tab, sin_tab, gain, *, batch, seq, blk=256):
    blk = min(blk, seq)
    nblk = seq // blk
    t = batch * seq
    return pl.pallas_call(
        functools.partial(_rt_kernel, blk=blk),
        grid=(batch, nblk),
        in_specs=[
            pl.BlockSpec((blk, 4 * RT_WIDTH), lambda b, s: (b * nblk + s, 3)),
            pl.BlockSpec((blk, HEAD_DIM), lambda b, s: (s, 0)),
            pl.BlockSpec((blk, HEAD_DIM), lambda b, s: (s, 0)),
            pl.BlockSpec((1, HEAD_DIM), lambda b, s: (0, 0)),
        ],
        out_specs=pl.BlockSpec((blk, RT_WIDTH), lambda b, s: (b * nblk + s, 0)),
        out_shape=jax.ShapeDtypeStruct((t, RT_WIDTH), BF16),
        scratch_shapes=[pltpu.VMEM((RT_HEADS, HEAD_DIM, HEAD_DIM), F32)],
        compiler_params=_params("parallel", "arbitrary"),
        name="retention",
    )(proj, cos_tab, sin_tab, gain)


def _outproj_kernel(h_ref, dn_ref, hg_ref, rt_ref, w_ref, o_ref):
    acc = _dot(dn_ref[...], w_ref[0:DN_WIDTH, :])
    acc += _dot(hg_ref[...], w_ref[DN_WIDTH:DN_WIDTH + HG_WIDTH, :])
    acc += _dot(rt_ref[...], w_ref[DN_WIDTH + HG_WIDTH:, :])
    o_ref[...] = h_ref[...] + acc


def _outproj(h, o_dn, o_hg, o_rt, w, *, tm=512):
    t, d = h.shape
    tm = min(tm, t)
    return pl.pallas_call(
        _outproj_kernel,
        grid=(t // tm,),
        in_specs=[
            pl.BlockSpec((tm, d), lambda i: (i, 0)),
            pl.BlockSpec((tm, DN_WIDTH), lambda i: (i, 0)),
            pl.BlockSpec((tm, HG_WIDTH), lambda i: (i, 0)),
            pl.BlockSpec((tm, RT_WIDTH), lambda i: (i, 0)),
            pl.BlockSpec((d, d), lambda i: (0, 0)),
        ],
        out_specs=pl.BlockSpec((tm, d), lambda i: (i, 0)),
        out_shape=jax.ShapeDtypeStruct((t, d), F32),
        compiler_params=_params("parallel"),
        name="outproj",
    )(h, o_dn, o_hg, o_rt, w)


def _ple_kernel(h_ref, nw_ref, wg_ref, p_ref, wp_ref, fn_ref, o_ref, *, final):
    x = h_ref[...]
    gate = _sigmoid(_dot(_bf(_rms_norm(x, nw_ref[...])), wg_ref[...]))
    y = x + gate * _dot(_bf(p_ref[...]), wp_ref[...])
    if final:
        y = _rms_norm(y, fn_ref[...])
    o_ref[...] = y


def _ple(h, nw, wg, p, wp, fn, *, final, tm=512):
    t, d = h.shape
    tm = min(tm, t)
    return pl.pallas_call(
        functools.partial(_ple_kernel, final=final),
        grid=(t // tm,),
        in_specs=[
            pl.BlockSpec((tm, d), lambda i: (i, 0)),
            pl.BlockSpec((1, d), lambda i: (0, 0)),
            pl.BlockSpec((d, d), lambda i: (0, 0)),
            pl.BlockSpec((tm, PLE_DIM), lambda i: (i, 0)),
            pl.BlockSpec((PLE_DIM, d), lambda i: (0, 0)),
            pl.BlockSpec((1, d), lambda i: (0, 0)),
        ],
        out_specs=pl.BlockSpec((tm, d), lambda i: (i, 0)),
        out_shape=jax.ShapeDtypeStruct((t, d), F32),
        compiler_params=_params("parallel"),
        name="ple",
    )(h, nw, wg, p, wp, fn)


def _regroup_w_in(w):
    c0 = 4 * DN_WIDTH
    c1 = c0 + 2 * DN_HEADS
    pad = jnp.zeros((w.shape[0], PROJ_WIDTH - PROJ_MAIN - 2 * DN_HEADS), w.dtype)
    return jnp.concatenate([w[:, :c0], w[:, c1:], w[:, c0:c1], pad], axis=1)


def _lane_row(x, offset):
    return jnp.zeros((1, LANES), F32).at[0, offset:offset + x.shape[0]].set(x.astype(F32))


def kernel(x, p, ffn1_norm, ffn1_w_gate_up, ffn1_w_down, mix_norm, w_in, dn_conv, dn_a_log, dn_dt_bias, dn_out_norm, hg_lower_bounds, hg_out_norm, rt_out_norm, w_out, ffn2_norm, ffn2_w_gate_up, ffn2_w_down, ple_norm, ple_w_gate, ple_w_proj, final_norm):
    batch, seq, d = x.shape
    t = batch * seq
    h = x.reshape(t, d)
    cos_tab, sin_tab = _rope_tables(seq)
    row = lambda v: v.reshape(1, -1).astype(F32)
    for i in range(DEPTH):
        h = _ffn(h, row(ffn1_norm[i]), _bf(ffn1_w_gate_up[i]), _bf(ffn1_w_down[i]))
        proj = _inproj(h, row(mix_norm[i]), _bf(_regroup_w_in(w_in[i])))
        o_dn = _deltanet(proj, dn_conv[i], _lane_row(dn_a_log[i], DN_HEADS), _lane_row(dn_dt_bias[i], DN_HEADS),
                         row(dn_out_norm[i]), batch=batch, seq=seq)
        o_hg = _hgrn2(proj, hg_lower_bounds, row(hg_out_norm[i]), layer=i, batch=batch, seq=seq)
        o_rt = _retention(proj, cos_tab, sin_tab, row(rt_out_norm[i]), batch=batch, seq=seq)
        h = _outproj(h, o_dn, o_hg, o_rt, _bf(w_out[i]))
        h = _ffn(h, row(ffn2_norm[i]), _bf(ffn2_w_gate_up[i]), _bf(ffn2_w_down[i]))
        h = _ple(h, row(ple_norm[i]), _bf(ple_w_gate[i]), p[i].reshape(t, PLE_DIM), _bf(ple_w_proj[i]),
                 row(final_norm), final=(i == DEPTH - 1))
    return h.reshape(batch, seq, d)
```

```python
import functools
import math

import jax
import jax.numpy as jnp
from jax import lax
from jax.experimental import pallas as pl
from jax.experimental.pallas import tpu as pltpu

F32 = jnp.float32
BF16 = jnp.bfloat16

D_MODEL = 2048
D_FF = 5632
DEPTH = 2
DN_HEADS = 8
HG_HEADS = 4
RT_HEADS = 4
HEAD_DIM = 128
DN_WIDTH = DN_HEADS * HEAD_DIM
HG_WIDTH = HG_HEADS * HEAD_DIM
RT_WIDTH = RT_HEADS * HEAD_DIM
QKV_WIDTH = 3 * DN_WIDTH
CONV_WIDTH = 4
PLE_DIM = 256
ROPE_THETA = 10000.0
NORM_EPS = 1e-6

LANES = 128
SUBLANES = 8
VMEM_LIMIT = 56 * 1024 * 1024

REST_MAIN = 4 * HG_WIDTH + 4 * RT_WIDTH + DN_WIDTH
REST_WIDTH = REST_MAIN + 256
REST_GATE_BLOCK = (4 * HG_WIDTH + 4 * RT_WIDTH) // DN_WIDTH
REST_SMALL_BLOCK = REST_MAIN // LANES

CHUNK = 128
HG_LEVELS = (64, 32, 16, 8, 4, 2, 1)


def _dot(a, b):
    return jnp.dot(a, b, preferred_element_type=F32)


def _dot_nt(a, b):
    return lax.dot_general(a, b, (((1,), (1,)), ((), ())), preferred_element_type=F32)


def _dot_tn(a, b):
    return lax.dot_general(a, b, (((0,), (0,)), ((), ())), preferred_element_type=F32)


def _bf(x):
    return x.astype(BF16)


def _sigmoid(x):
    return 1.0 / (1.0 + jnp.exp(-x))


def _silu(x):
    return x * _sigmoid(x)


def _rms_norm(x, w):
    return x * lax.rsqrt(jnp.mean(x * x, axis=-1, keepdims=True) + NORM_EPS) * w


def _split_bf16(x):
    hi = x.astype(BF16)
    lo = (x - hi.astype(F32)).astype(BF16)
    return hi, lo


def _params(*sem):
    return pltpu.CompilerParams(dimension_semantics=sem, vmem_limit_bytes=VMEM_LIMIT)


def _ffn_kernel(h_ref, nw_ref, wg_ref, wu_ref, wd_ref, o_ref, hn_ref, acc_ref, *, nf):
    j = pl.program_id(1)

    @pl.when(j == 0)
    def _():
        hn_ref[...] = _bf(_rms_norm(h_ref[...], nw_ref[...]))
        acc_ref[...] = jnp.zeros_like(acc_ref)

    hn = hn_ref[...]
    g = _dot(hn, wg_ref[...])
    u = _dot(hn, wu_ref[...])
    acc_ref[...] += _dot(_bf(_silu(g) * u), wd_ref[...])

    @pl.when(j == nf - 1)
    def _():
        o_ref[...] = h_ref[...] + 0.5 * acc_ref[...]


def _ffn(h, nw, wgu, wd, *, tm=512, tf=512):
    t, d = h.shape
    tm = min(tm, t)
    nf = D_FF // tf
    return pl.pallas_call(
        functools.partial(_ffn_kernel, nf=nf),
        grid=(t // tm, nf),
        in_specs=[
            pl.BlockSpec((tm, d), lambda i, j: (i, 0)),
            pl.BlockSpec((1, d), lambda i, j: (0, 0)),
            pl.BlockSpec((d, tf), lambda i, j: (0, j)),
            pl.BlockSpec((d, tf), lambda i, j: (0, j + nf)),
            pl.BlockSpec((tf, d), lambda i, j: (j, 0)),
        ],
        out_specs=pl.BlockSpec((tm, d), lambda i, j: (i, 0)),
        out_shape=jax.ShapeDtypeStruct((t, d), F32),
        scratch_shapes=[pltpu.VMEM((tm, d), BF16), pltpu.VMEM((tm, d), F32)],
        compiler_params=_params("parallel", "arbitrary"),
        name="ffn",
    )(h, nw, wgu, wgu, wd)


def _qkv_kernel(h_ref, nw_ref, w_ref, cw_ref, o_ref, hn_ref, tail_ref, *, tiles_per_seq, cb):
    i = pl.program_id(0)
    first = (i % tiles_per_seq) == 0
    tm = h_ref.shape[0]
    hn_ref[...] = _bf(_rms_norm(h_ref[...], nw_ref[...]))
    for c in range(QKV_WIDTH // cb):
        cs = slice(c * cb, (c + 1) * cb)
        y = _dot(hn_ref[...], w_ref[:, cs])
        tail = jnp.where(first, 0.0, tail_ref[:, cs])
        tail_ref[:, cs] = y[tm - SUBLANES:, :]
        z = jnp.concatenate([tail, y], axis=0)
        acc = y * cw_ref[CONV_WIDTH - 1:CONV_WIDTH, cs]
        for j in range(CONV_WIDTH - 1):
            shifted = pltpu.roll(z, CONV_WIDTH - 1 - j, axis=0)[SUBLANES:, :]
            acc = acc + shifted * cw_ref[j:j + 1, cs]
        x = _silu(acc)
        for hh in range(cb // HEAD_DIM):
            col = c * cb + hh * HEAD_DIM
            xh = x[:, hh * HEAD_DIM:(hh + 1) * HEAD_DIM]
            if col < 2 * DN_WIDTH:
                r = lax.rsqrt(jnp.sum(xh * xh, axis=-1, keepdims=True) + NORM_EPS)
                if col < DN_WIDTH:
                    r = r * (HEAD_DIM ** -0.5)
                xh = xh * r
            o_ref[:, col:col + HEAD_DIM] = xh


def _qkvproj(h, nw, w, conv_w, *, seq, tm=512, cb=256):
    t, d = h.shape
    tm = min(tm, seq)
    return pl.pallas_call(
        functools.partial(_qkv_kernel, tiles_per_seq=seq // tm, cb=cb),
        grid=(t // tm,),
        in_specs=[
            pl.BlockSpec((tm, d), lambda i: (i, 0)),
            pl.BlockSpec((1, d), lambda i: (0, 0)),
            pl.BlockSpec((d, QKV_WIDTH), lambda i: (0, 0)),
            pl.BlockSpec((CONV_WIDTH, QKV_WIDTH), lambda i: (0, 0)),
        ],
        out_specs=pl.BlockSpec((tm, QKV_WIDTH), lambda i: (i, 0)),
        out_shape=jax.ShapeDtypeStruct((t, QKV_WIDTH), F32),
        scratch_shapes=[pltpu.VMEM((tm, d), BF16), pltpu.VMEM((SUBLANES, QKV_WIDTH), F32)],
        compiler_params=_params("arbitrary"),
        name="qkvproj",
    )(h, nw, w, conv_w)


def _restproj_kernel(h_ref, nw_ref, w_ref, o_ref, hn_ref):
    @pl.when(pl.program_id(1) == 0)
    def _():
        hn_ref[...] = _bf(_rms_norm(h_ref[...], nw_ref[...]))

    o_ref[...] = _dot(hn_ref[...], w_ref[...])


def _restproj(h, nw, w, *, tm=512, tn=1792):
    t, d = h.shape
    tm = min(tm, t)
    n = w.shape[1]
    return pl.pallas_call(
        _restproj_kernel,
        grid=(t // tm, n // tn),
        in_specs=[
            pl.BlockSpec((tm, d), lambda i, j: (i, 0)),
            pl.BlockSpec((1, d), lambda i, j: (0, 0)),
            pl.BlockSpec((d, tn), lambda i, j: (0, j)),
        ],
        out_specs=pl.BlockSpec((tm, tn), lambda i, j: (i, j)),
        out_shape=jax.ShapeDtypeStruct((t, n), F32),
        scratch_shapes=[pltpu.VMEM((tm, d), BF16)],
        compiler_params=_params("parallel", "arbitrary"),
        name="restproj",
    )(h, nw, w)


def _chunk_tril_bf16(blk):
    r = lax.broadcasted_iota(jnp.int32, (blk, blk), 0)
    c = lax.broadcasted_iota(jnp.int32, (blk, blk), 1)
    return jnp.where((r // CHUNK == c // CHUNK) & (c <= r), 1.0, 0.0).astype(BF16)


def _head_norm_gate(o, gain, gate):
    o = o * lax.rsqrt(jnp.mean(o * o, axis=-1, keepdims=True) + NORM_EPS) * gain
    return _bf(o * _silu(gate))


def _dn_kernel(qkv_ref, gate_ref, small_ref, alog_ref, dtb_ref, gain_ref, o_ref, state_ref, *, blk, group):
    s = pl.program_id(1)
    nc = blk // CHUNK
    half = CHUNK // 2

    @pl.when(s == 0)
    def _():
        state_ref[...] = jnp.zeros_like(state_ref)

    small = small_ref[...]
    beta_all = _sigmoid(small)
    x = small + dtb_ref[...]
    softplus = jnp.maximum(x, 0.0) + jnp.log(1.0 + jnp.exp(-jnp.abs(x)))
    g_all = -(jnp.exp(alog_ref[...]) * softplus)
    tril = _chunk_tril_bf16(blk)
    g_hi, g_lo = _split_bf16(g_all)
    gc_all = _dot(tril, g_hi) + _dot(tril, g_lo)
    gc_t = gc_all.T

    row = lax.broadcasted_iota(jnp.int32, (CHUNK, CHUNK), 0)
    col = lax.broadcasted_iota(jnp.int32, (CHUNK, CHUNK), 1)
    incl = col <= row
    same_half = (row // half) == (col // half)
    diag_blocks = same_half & (col < row)
    off_block = (row >= half) & (col < half)

    for g0 in range(0, DN_HEADS, group):
        heads = list(range(g0, g0 + group))
        units = [(h, c) for c in range(nc) for h in heads]

        hd = {}
        for h in heads:
            q = qkv_ref[:, h * HEAD_DIM:(h + 1) * HEAD_DIM]
            k = qkv_ref[:, DN_WIDTH + h * HEAD_DIM:DN_WIDTH + (h + 1) * HEAD_DIM]
            v = qkv_ref[:, 2 * DN_WIDTH + h * HEAD_DIM:2 * DN_WIDTH + (h + 1) * HEAD_DIM]
            beta = jnp.broadcast_to(beta_all[:, h:h + 1], (blk, HEAD_DIM))
            gc = jnp.broadcast_to(gc_all[:, DN_HEADS + h:DN_HEADS + h + 1], (blk, HEAD_DIM))
            eg = jnp.exp(gc)
            kbn = -(k * beta)
            hd[h] = dict(q=q, k=k, kbn=kbn, gc=gc, kb=_bf(k),
                         rhs=_bf(jnp.concatenate([v * beta, kbn * eg], axis=1)),
                         qeg=_bf(q * eg), gc_row=gc_t[DN_HEADS + h:DN_HEADS + h + 1, :])

        ln_bd, ln_off, qk, k_dec = {}, {}, {}, {}
        for (h, c) in units:
            d = hd[h]
            sl = slice(c * CHUNK, (c + 1) * CHUNK)
            gcc = d["gc"][sl]
            decay = jnp.exp(jnp.where(incl, gcc - d["gc_row"][:, sl], -jnp.inf))
            a = _dot_nt(_bf(jnp.concatenate([d["kbn"][sl], d["q"][sl]], axis=0)), d["kb"][sl])
            ln = a[:CHUNK] * decay
            ln_bd[h, c] = jnp.where(diag_blocks, ln, 0.0)
            ln_off[h, c] = jnp.where(off_block, ln, 0.0)
            qk[h, c] = _bf(a[CHUNK:] * decay)
            g_last = gcc[CHUNK - 1:CHUNK, :]
            k_dec[h, c] = _bf(d["k"][sl] * jnp.exp(g_last - gcc))

        p = dict(ln_bd)
        m = {}
        for u in units:
            b = _bf(ln_bd[u])
            m[u] = _dot(b, b)
        for _ in range(int(math.log2(half)) - 2):
            for u in units:
                mb = _bf(m[u])
                x2 = _dot(jnp.concatenate([mb, _bf(p[u])], axis=0), mb)
                p[u] = p[u] + m[u] + x2[CHUNK:]
                m[u] = x2[:CHUNK]
        for u in units:
            p[u] = p[u] + m[u] + _dot(_bf(p[u]), _bf(m[u]))
        for u in units:
            pb = _bf(p[u])
            y = ln_off[u] + _dot(_bf(ln_off[u]), pb)
            p[u] = p[u] + y + _dot(pb, _bf(y))
        sol = {}
        for (h, c) in units:
            rhs = hd[h]["rhs"][c * CHUNK:(c + 1) * CHUNK]
            sol[h, c] = rhs.astype(F32) + _dot(_bf(p[h, c]), rhs)

        states = {h: state_ref[h] for h in heads}
        outs = {h: [] for h in heads}
        for c in range(nc):
            sl = slice(c * CHUNK, (c + 1) * CHUNK)
            for h in heads:
                d = hd[h]
                u_ = sol[h, c][:, :HEAD_DIM]
                wn = sol[h, c][:, HEAD_DIM:]
                ws = _dot(jnp.concatenate([_bf(wn), d["qeg"][sl]], axis=0), _bf(states[h]))
                v_new = _bf(u_ + ws[:CHUNK])
                outs[h].append(ws[CHUNK:] + _dot(qk[h, c], v_new))
                g_last = d["gc"][(c + 1) * CHUNK - 1:(c + 1) * CHUNK, :]
                states[h] = states[h] * jnp.exp(g_last) + _dot_tn(k_dec[h, c], v_new)
        for h in heads:
            state_ref[h] = states[h]
            o = jnp.concatenate(outs[h], axis=0) if nc > 1 else outs[h][0]
            cols = slice(h * HEAD_DIM, (h + 1) * HEAD_DIM)
            o_ref[:, cols] = _head_norm_gate(o, gain_ref[...], gate_ref[:, cols])


def _deltanet(qkv, rest, alog_row, dtb_row, gain, *, batch, seq, blk=256, group=4):
    blk = min(blk, seq)
    nblk = seq // blk
    t = batch * seq
    row = lambda b, s: (b * nblk + s)
    return pl.pallas_call(
        functools.partial(_dn_kernel, blk=blk, group=group),
        grid=(batch, nblk),
        in_specs=[
            pl.BlockSpec((blk, QKV_WIDTH), lambda b, s: (row(b, s), 0)),
            pl.BlockSpec((blk, DN_WIDTH), lambda b, s: (row(b, s), REST_GATE_BLOCK)),
            pl.BlockSpec((blk, LANES), lambda b, s: (row(b, s), REST_SMALL_BLOCK)),
            pl.BlockSpec((1, LANES), lambda b, s: (0, 0)),
            pl.BlockSpec((1, LANES), lambda b, s: (0, 0)),
            pl.BlockSpec((1, HEAD_DIM), lambda b, s: (0, 0)),
        ],
        out_specs=pl.BlockSpec((blk, DN_WIDTH), lambda b, s: (row(b, s), 0)),
        out_shape=jax.ShapeDtypeStruct((t, DN_WIDTH), BF16),
        scratch_shapes=[pltpu.VMEM((DN_HEADS, HEAD_DIM, HEAD_DIM), F32)],
        compiler_params=_params("parallel", "arbitrary"),
        name="deltanet",
    )(qkv, rest, rest, alog_row, dtb_row, gain)


def _hg_level_matrix():
    r = lax.broadcasted_iota(jnp.int32, (CHUNK, CHUNK), 0)
    c = lax.broadcasted_iota(jnp.int32, (CHUNK, CHUNK), 1)
    mats = [c <= r]
    for m in HG_LEVELS:
        mats.append(c <= (r // (2 * m)) * (2 * m) + m - 1)
    return jnp.concatenate([jnp.where(x, 1.0, 0.0).astype(BF16) for x in mats], axis=0)


def _hg_kernel(x_ref, lbt_ref, gain_ref, o_ref, state_ref, *, blk, layer):
    s = pl.program_id(1)

    @pl.when(s == 0)
    def _():
        state_ref[...] = jnp.zeros_like(state_ref)

    tbl = lbt_ref[...]
    e = jnp.exp(tbl - jnp.max(tbl, axis=0, keepdims=True))
    sm = e / jnp.sum(e, axis=0, keepdims=True)
    lb = jnp.zeros((1, HG_WIDTH), F32)
    for l in range(1, layer + 1):
        lb = lb + sm[l:l + 1, :]

    level_mat = _hg_level_matrix()
    row = lax.broadcasted_iota(jnp.int32, (CHUNK, CHUNK), 0)
    col = lax.broadcasted_iota(jnp.int32, (CHUNK, CHUNK), 1)
    rowv = lax.broadcasted_iota(jnp.int32, (CHUNK, HEAD_DIM), 0)

    for c in range(blk // CHUNK):
        sl = slice(c * CHUNK, (c + 1) * CHUNK)
        z = x_ref[sl, HG_WIDTH:2 * HG_WIDTH]
        log_f = jnp.log(lb + (1.0 - lb) * _sigmoid(z))
        hk_all = (1.0 - lb) * _sigmoid(-z)
        lf_hi, lf_lo = _split_bf16(log_f)
        y = _dot(level_mat, jnp.concatenate([lf_hi, lf_lo], axis=1))
        cums = y[:, :HG_WIDTH] + y[:, HG_WIDTH:]
        for h in range(HG_HEADS):
            cols = slice(h * HEAD_DIM, (h + 1) * HEAD_DIM)
            q = _silu(x_ref[sl, cols]) * (HEAD_DIM ** -0.5)
            k = hk_all[:, cols]
            v = x_ref[sl, 2 * HG_WIDTH + h * HEAD_DIM:2 * HG_WIDTH + (h + 1) * HEAD_DIM]
            b = cums[0:CHUNK, cols]
            att = jnp.zeros((CHUNK, CHUNK), F32)
            for li, m in enumerate(HG_LEVELS):
                ref = cums[(li + 1) * CHUNK:(li + 2) * CHUNK, cols]
                lower = (rowv // m) % 2 == 1
                ex = jnp.exp(jnp.where(lower, b - ref, ref - b))
                qt = jnp.where(lower, q * ex, 0.0)
                kt = jnp.where(lower, 0.0, k * ex)
                same_group = (row // (2 * m)) == (col // (2 * m))
                att = att + jnp.where(same_group, _dot_nt(_bf(qt), _bf(kt)), 0.0)
            diag = jnp.sum(q * k, axis=-1, keepdims=True)
            state_t = state_ref[h]
            o = (_dot(_bf(att), _bf(v)) + diag * v
                 + _dot_nt(_bf(q * jnp.exp(b)), _bf(state_t)))
            b_last = b[CHUNK - 1:CHUNK, :]
            k_dec = k * jnp.exp(b_last - b)
            state_ref[h] = state_t * jnp.exp(b_last) + _dot_tn(_bf(v), _bf(k_dec))
            gate = x_ref[sl, 3 * HG_WIDTH + h * HEAD_DIM:3 * HG_WIDTH + (h + 1) * HEAD_DIM]
            o_ref[sl, cols] = _head_norm_gate(o, gain_ref[...], gate)


def _hgrn2(rest, lb_table, gain, *, layer, batch, seq, blk=256):
    blk = min(blk, seq)
    nblk = seq // blk
    t = batch * seq
    return pl.pallas_call(
        functools.partial(_hg_kernel, blk=blk, layer=layer),
        grid=(batch, nblk),
        in_specs=[
            pl.BlockSpec((blk, 4 * HG_WIDTH), lambda b, s: (b * nblk + s, 0)),
            pl.BlockSpec((DEPTH, HG_WIDTH), lambda b, s: (0, 0)),
            pl.BlockSpec((1, HEAD_DIM), lambda b, s: (0, 0)),
        ],
        out_specs=pl.BlockSpec((blk, HG_WIDTH), lambda b, s: (b * nblk + s, 0)),
        out_shape=jax.ShapeDtypeStruct((t, HG_WIDTH), BF16),
        scratch_shapes=[pltpu.VMEM((HG_HEADS, HEAD_DIM, HEAD_DIM), F32)],
        compiler_params=_params("parallel", "arbitrary"),
        name="hgrn2",
    )(rest, lb_table, gain)


def _rope_kernel(cos_ref, sin_ref):
    shape = cos_ref.shape
    half = HEAD_DIM // 2
    pos = lax.broadcasted_iota(jnp.int32, shape, 0).astype(F32)
    lane = lax.broadcasted_iota(jnp.int32, shape, 1)
    inv_freq = jnp.exp((lane % half).astype(F32) * (-math.log(ROPE_THETA) / half))
    ang = pos * inv_freq
    cos_ref[...] = jnp.cos(ang)
    sin = jnp.sin(ang)
    sin_ref[...] = jnp.where(lane < half, -sin, sin)


def _rope_tables(seq):
    shape = jax.ShapeDtypeStruct((seq, HEAD_DIM), F32)
    return pl.pallas_call(_rope_kernel, out_shape=(shape, shape), name="rope_tables")()


def _rt_kernel(x_ref, cos_ref, sin_ref, gain_ref, o_ref, state_ref, *, blk):
    s = pl.program_id(1)

    @pl.when(s == 0)
    def _():
        state_ref[...] = jnp.zeros_like(state_ref)

    cos = cos_ref[...]
    sin = sin_ref[...]
    row = lax.broadcasted_iota(jnp.int32, (blk, blk), 0)
    col = lax.broadcasted_iota(jnp.int32, (blk, blk), 1)
    dist = (row - col).astype(F32)
    pos = lax.broadcasted_iota(jnp.int32, (blk, HEAD_DIM), 0).astype(F32)

    def rotary(x):
        return x * cos + pltpu.roll(x, HEAD_DIM // 2, axis=1) * sin

    for h in range(RT_HEADS):
        log_gamma = math.log(1.0 - 2.0 ** (-5.0 - h))
        cols = slice(h * HEAD_DIM, (h + 1) * HEAD_DIM)
        q = rotary(x_ref[:, cols]) * (HEAD_DIM ** -0.5)
        k = rotary(x_ref[:, RT_WIDTH + h * HEAD_DIM:RT_WIDTH + (h + 1) * HEAD_DIM])
        v = x_ref[:, 2 * RT_WIDTH + h * HEAD_DIM:2 * RT_WIDTH + (h + 1) * HEAD_DIM]
        vb = _bf(v)
        dmat = jnp.exp(jnp.where(col <= row, dist * log_gamma, -jnp.inf))
        att = _dot_nt(_bf(q), _bf(k)) * dmat
        state = state_ref[h]
        o = _dot(_bf(att), vb) + _dot(_bf(q * jnp.exp((pos + 1.0) * log_gamma)), _bf(state))
        k_dec = k * jnp.exp((blk - 1.0 - pos) * log_gamma)
        state_ref[h] = state * math.exp(blk * log_gamma) + _dot_tn(_bf(k_dec), vb)
        gate = x_ref[:, 3 * RT_WIDTH + h * HEAD_DIM:3 * RT_WIDTH + (h + 1) * HEAD_DIM]
        o_ref[:, cols] = _head_norm_gate(o, gain_ref[...], gate)


def _retention(rest, cos_tab, sin_tab, gain, *, batch, seq, blk=256):
    blk = min(blk, seq)
    nblk = seq // blk
    t = batch * seq
    return pl.pallas_call(
        functools.partial(_rt_kernel, blk=blk),
        grid=(batch, nblk),
        in_specs=[
            pl.BlockSpec((blk, 4 * RT_WIDTH), lambda b, s: (b * nblk + s, 1)),
            pl.BlockSpec((blk, HEAD_DIM), lambda b, s: (s, 0)),
            pl.BlockSpec((blk, HEAD_DIM), lambda b, s: (s, 0)),
            pl.BlockSpec((1, HEAD_DIM), lambda b, s: (0, 0)),
        ],
        out_specs=pl.BlockSpec((blk, RT_WIDTH), lambda b, s: (b * nblk + s, 0)),
        out_shape=jax.ShapeDtypeStruct((t, RT_WIDTH), BF16),
        scratch_shapes=[pltpu.VMEM((RT_HEADS, HEAD_DIM, HEAD_DIM), F32)],
        compiler_params=_params("parallel", "arbitrary"),
        name="retention",
    )(rest, cos_tab, sin_tab, gain)


def _outproj_kernel(h_ref, dn_ref, hg_ref, rt_ref, w_ref, o_ref):
    acc = _dot(dn_ref[...], w_ref[0:DN_WIDTH, :])
    acc += _dot(hg_ref[...], w_ref[DN_WIDTH:DN_WIDTH + HG_WIDTH, :])
    acc += _dot(rt_ref[...], w_ref[DN_WIDTH + HG_WIDTH:, :])
    o_ref[...] = h_ref[...] + acc


def _outproj(h, o_dn, o_hg, o_rt, w, *, tm=512):
    t, d = h.shape
    tm = min(tm, t)
    return pl.pallas_call(
        _outproj_kernel,
        grid=(t // tm,),
        in_specs=[
            pl.BlockSpec((tm, d), lambda i: (i, 0)),
            pl.BlockSpec((tm, DN_WIDTH), lambda i: (i, 0)),
            pl.BlockSpec((tm, HG_WIDTH), lambda i: (i, 0)),
            pl.BlockSpec((tm, RT_WIDTH), lambda i: (i, 0)),
            pl.BlockSpec((d, d), lambda i: (0, 0)),
        ],
        out_specs=pl.BlockSpec((tm, d), lambda i: (i, 0)),
        out_shape=jax.ShapeDtypeStruct((t, d), F32),
        compiler_params=_params("parallel"),
        name="outproj",
    )(h, o_dn, o_hg, o_rt, w)


def _ple_kernel(h_ref, nw_ref, wg_ref, p_ref, wp_ref, fn_ref, o_ref, *, final):
    x = h_ref[...]
    gate = _sigmoid(_dot(_bf(_rms_norm(x, nw_ref[...])), wg_ref[...]))
    y = x + gate * _dot(_bf(p_ref[...]), wp_ref[...])
    if final:
        y = _rms_norm(y, fn_ref[...])
    o_ref[...] = y


def _ple(h, nw, wg, p, wp, fn, *, final, tm=512):
    t, d = h.shape
    tm = min(tm, t)
    return pl.pallas_call(
        functools.partial(_ple_kernel, final=final),
        grid=(t // tm,),
        in_specs=[
            pl.BlockSpec((tm, d), lambda i: (i, 0)),
            pl.BlockSpec((1, d), lambda i: (0, 0)),
            pl.BlockSpec((d, d), lambda i: (0, 0)),
            pl.BlockSpec((tm, PLE_DIM), lambda i: (i, 0)),
            pl.BlockSpec((PLE_DIM, d), lambda i: (0, 0)),
            pl.BlockSpec((1, d), lambda i: (0, 0)),
        ],
        out_specs=pl.BlockSpec((tm, d), lambda i: (i, 0)),
        out_shape=jax.ShapeDtypeStruct((t, d), F32),
        compiler_params=_params("parallel"),
        name="ple",
    )(h, nw, wg, p, wp, fn)


def _split_w_in(w):
    c0 = QKV_WIDTH
    c1 = c0 + DN_WIDTH
    c2 = c1 + 2 * DN_HEADS
    pad = jnp.zeros((w.shape[0], REST_WIDTH - REST_MAIN - 2 * DN_HEADS), w.dtype)
    return w[:, :c0], jnp.concatenate([w[:, c2:], w[:, c0:c1], w[:, c1:c2], pad], axis=1)


def _lane_row(x, offset):
    return jnp.zeros((1, LANES), F32).at[0, offset:offset + x.shape[0]].set(x.astype(F32))


def kernel(x, p, ffn1_norm, ffn1_w_gate_up, ffn1_w_down, mix_norm, w_in, dn_conv, dn_a_log, dn_dt_bias, dn_out_norm, hg_lower_bounds, hg_out_norm, rt_out_norm, w_out, ffn2_norm, ffn2_w_gate_up, ffn2_w_down, ple_norm, ple_w_gate, ple_w_proj, final_norm):
    batch, seq, d = x.shape
    t = batch * seq
    h = x.reshape(t, d)
    cos_tab, sin_tab = _rope_tables(seq)
    row = lambda v: v.reshape(1, -1).astype(F32)
    for i in range(DEPTH):
        h = _ffn(h, row(ffn1_norm[i]), _bf(ffn1_w_gate_up[i]), _bf(ffn1_w_down[i]))
        w_qkv, w_rest = _split_w_in(w_in[i])
        qkv = _qkvproj(h, row(mix_norm[i]), _bf(w_qkv), dn_conv[i], seq=seq)
        rest = _restproj(h, row(mix_norm[i]), _bf(w_rest))
        o_dn = _deltanet(qkv, rest, _lane_row(dn_a_log[i], DN_HEADS), _lane_row(dn_dt_bias[i], DN_HEADS),
                         row(dn_out_norm[i]), batch=batch, seq=seq)
        o_hg = _hgrn2(rest, hg_lower_bounds, row(hg_out_norm[i]), layer=i, batch=batch, seq=seq)
        o_rt = _retention(rest, cos_tab, sin_tab, row(rt_out_norm[i]), batch=batch, seq=seq)
        h = _outproj(h, o_dn, o_hg, o_rt, _bf(w_out[i]))
        h = _ffn(h, row(ffn2_norm[i]), _bf(ffn2_w_gate_up[i]), _bf(ffn2_w_down[i]))
        h = _ple(h, row(ple_norm[i]), _bf(ple_w_gate[i]), p[i].reshape(t, PLE_DIM), _bf(ple_w_proj[i]),
                 row(final_norm), final=(i == DEPTH - 1))
    return h.reshape(batch, seq, d)
```

```python
import functools
import math

import jax
import jax.numpy as jnp
from jax import lax
from jax.experimental import pallas as pl
from jax.experimental.pallas import tpu as pltpu

F32 = jnp.float32
BF16 = jnp.bfloat16

D_MODEL = 2048
D_FF = 5632
DEPTH = 2
DN_HEADS = 8
HG_HEADS = 4
RT_HEADS = 4
HEAD_DIM = 128
DN_WIDTH = DN_HEADS * HEAD_DIM
HG_WIDTH = HG_HEADS * HEAD_DIM
RT_WIDTH = RT_HEADS * HEAD_DIM
QKV_WIDTH = 3 * DN_WIDTH
CONV_WIDTH = 4
PLE_DIM = 256
ROPE_THETA = 10000.0
NORM_EPS = 1e-6
LOG2_E = math.log2(math.e)

LANES = 128
SUBLANES = 8
VMEM_LIMIT = 56 * 1024 * 1024

REST_MAIN = 4 * HG_WIDTH + 4 * RT_WIDTH + DN_WIDTH
REST_WIDTH = REST_MAIN + 256
REST_GATE_BLOCK = (4 * HG_WIDTH + 4 * RT_WIDTH) // DN_WIDTH
REST_SMALL_BLOCK = REST_MAIN // LANES

CHUNK = 128
HG_LEVELS = (64, 32, 16, 8, 4, 2, 1)


def _dot(a, b):
    return jnp.dot(a, b, preferred_element_type=F32)


def _dot_nt(a, b):
    return lax.dot_general(a, b, (((1,), (1,)), ((), ())), preferred_element_type=F32)


def _dot_tn(a, b):
    return lax.dot_general(a, b, (((0,), (0,)), ((), ())), preferred_element_type=F32)


def _bf(x):
    return x.astype(BF16)


def _sigmoid(x):
    return 1.0 / (1.0 + jnp.exp(-x))


def _silu(x):
    return x * _sigmoid(x)


def _rms_norm(x, w):
    return x * lax.rsqrt(jnp.mean(x * x, axis=-1, keepdims=True) + NORM_EPS) * w


def _split_bf16(x):
    hi = x.astype(BF16)
    lo = (x - hi.astype(F32)).astype(BF16)
    return hi, lo


def _params(*sem):
    return pltpu.CompilerParams(dimension_semantics=sem, vmem_limit_bytes=VMEM_LIMIT)


def _ffn_kernel(h_ref, nw_ref, wg_ref, wu_ref, wd_ref, o_ref, hn_ref, acc_ref, *, nf):
    j = pl.program_id(1)

    @pl.when(j == 0)
    def _():
        hn_ref[...] = _bf(_rms_norm(h_ref[...], nw_ref[...]))
        acc_ref[...] = jnp.zeros_like(acc_ref)

    hn = hn_ref[...]
    g = _dot(hn, wg_ref[...])
    u = _dot(hn, wu_ref[...])
    acc_ref[...] += _dot(_bf(_silu(g) * u), wd_ref[...])

    @pl.when(j == nf - 1)
    def _():
        o_ref[...] = h_ref[...] + 0.5 * acc_ref[...]


def _ffn(h, nw, wgu, wd, *, layer, tm=512, tf=512):
    t, d = h.shape
    tm = min(tm, t)
    nf = D_FF // tf
    return pl.pallas_call(
        functools.partial(_ffn_kernel, nf=nf),
        grid=(t // tm, nf),
        in_specs=[
            pl.BlockSpec((tm, d), lambda i, j: (i, 0)),
            pl.BlockSpec((None, 1, d), lambda i, j: (layer, 0, 0)),
            pl.BlockSpec((None, d, tf), lambda i, j: (layer, 0, j)),
            pl.BlockSpec((None, d, tf), lambda i, j: (layer, 0, j + nf)),
            pl.BlockSpec((None, tf, d), lambda i, j: (layer, j, 0)),
        ],
        out_specs=pl.BlockSpec((tm, d), lambda i, j: (i, 0)),
        out_shape=jax.ShapeDtypeStruct((t, d), F32),
        scratch_shapes=[pltpu.VMEM((tm, d), BF16), pltpu.VMEM((tm, d), F32)],
        compiler_params=_params("parallel", "arbitrary"),
        name="ffn",
    )(h, nw, wgu, wgu, wd)


def _qkv_kernel(h_ref, nw_ref, w_ref, cw_ref, o_ref, hn_ref, tail_ref, *, tiles_per_seq, cb):
    i = pl.program_id(0)
    first = (i % tiles_per_seq) == 0
    tm = h_ref.shape[0]
    hn_ref[...] = _bf(_rms_norm(h_ref[...], nw_ref[...]))
    for c in range(QKV_WIDTH // cb):
        cs = slice(c * cb, (c + 1) * cb)
        y = _dot(hn_ref[...], w_ref[:, cs])
        tail = jnp.where(first, 0.0, tail_ref[:, cs])
        tail_ref[:, cs] = y[tm - SUBLANES:, :]
        z = jnp.concatenate([tail, y], axis=0)
        acc = y * cw_ref[CONV_WIDTH - 1:CONV_WIDTH, cs]
        for j in range(CONV_WIDTH - 1):
            shifted = pltpu.roll(z, CONV_WIDTH - 1 - j, axis=0)[SUBLANES:, :]
            acc = acc + shifted * cw_ref[j:j + 1, cs]
        x = _silu(acc)
        for hh in range(cb // HEAD_DIM):
            col = c * cb + hh * HEAD_DIM
            xh = x[:, hh * HEAD_DIM:(hh + 1) * HEAD_DIM]
            if col < 2 * DN_WIDTH:
                r = lax.rsqrt(jnp.sum(xh * xh, axis=-1, keepdims=True) + NORM_EPS)
                if col < DN_WIDTH:
                    r = r * (HEAD_DIM ** -0.5)
                xh = xh * r
            o_ref[:, col:col + HEAD_DIM] = xh


def _qkvproj(h, nw, w, conv_w, *, layer, seq, tm=512, cb=256):
    t, d = h.shape
    tm = min(tm, seq)
    return pl.pallas_call(
        functools.partial(_qkv_kernel, tiles_per_seq=seq // tm, cb=cb),
        grid=(t // tm,),
        in_specs=[
            pl.BlockSpec((tm, d), lambda i: (i, 0)),
            pl.BlockSpec((None, 1, d), lambda i: (layer, 0, 0)),
            pl.BlockSpec((None, d, QKV_WIDTH), lambda i: (layer, 0, 0)),
            pl.BlockSpec((None, CONV_WIDTH, QKV_WIDTH), lambda i: (layer, 0, 0)),
        ],
        out_specs=pl.BlockSpec((tm, QKV_WIDTH), lambda i: (i, 0)),
        out_shape=jax.ShapeDtypeStruct((t, QKV_WIDTH), F32),
        scratch_shapes=[pltpu.VMEM((tm, d), BF16), pltpu.VMEM((SUBLANES, QKV_WIDTH), F32)],
        compiler_params=_params("arbitrary"),
        name="qkvproj",
    )(h, nw, w, conv_w)


def _restproj_kernel(h_ref, nw_ref, w_ref, o_ref, hn_ref):
    @pl.when(pl.program_id(1) == 0)
    def _():
        hn_ref[...] = _bf(_rms_norm(h_ref[...], nw_ref[...]))

    o_ref[...] = _dot(hn_ref[...], w_ref[...])


def _restproj(h, nw, w, *, layer, tm=1024, tn=1792):
    t, d = h.shape
    tm = min(tm, t)
    n = w.shape[-1]
    return pl.pallas_call(
        _restproj_kernel,
        grid=(t // tm, n // tn),
        in_specs=[
            pl.BlockSpec((tm, d), lambda i, j: (i, 0)),
            pl.BlockSpec((None, 1, d), lambda i, j: (layer, 0, 0)),
            pl.BlockSpec((None, d, tn), lambda i, j: (layer, 0, j)),
        ],
        out_specs=pl.BlockSpec((tm, tn), lambda i, j: (i, j)),
        out_shape=jax.ShapeDtypeStruct((t, n), F32),
        scratch_shapes=[pltpu.VMEM((tm, d), BF16)],
        compiler_params=_params("parallel", "arbitrary"),
        name="restproj",
    )(h, nw, w)


def _chunk_tril_bf16(blk):
    r = lax.broadcasted_iota(jnp.int32, (blk, blk), 0)
    c = lax.broadcasted_iota(jnp.int32, (blk, blk), 1)
    return jnp.where((r // CHUNK == c // CHUNK) & (c <= r), 1.0, 0.0).astype(BF16)


def _cat1(*xs):
    return jnp.concatenate(xs, axis=1)


def _bd(a, b):
    z = jnp.zeros_like(a)
    return jnp.concatenate([_cat1(a, z), _cat1(z, b)], axis=0)


def _head_norm_gate(o, gain, gate):
    o = o * lax.rsqrt(jnp.mean(o * o, axis=-1, keepdims=True) + NORM_EPS) * gain
    return _bf(o * _silu(gate))


def _dn_kernel(qkv_ref, gate_ref, small_ref, alog_ref, dtb_ref, gain_ref, o_ref, state_ref, *, blk, group):
    s = pl.program_id(1)
    nc = blk // CHUNK
    half = CHUNK // 2

    @pl.when(s == 0)
    def _():
        state_ref[...] = jnp.zeros_like(state_ref)

    small = small_ref[...]
    beta_all = _sigmoid(small)
    x = small + dtb_ref[...]
    softplus = jnp.maximum(x, 0.0) + jnp.log(1.0 + jnp.exp(-jnp.abs(x)))
    g_all = -(jnp.exp(alog_ref[...]) * softplus)
    tril = _chunk_tril_bf16(blk)
    g_hi, g_lo = _split_bf16(g_all)
    gc_all = _dot(tril, g_hi) + _dot(tril, g_lo)
    gc_t = gc_all.T

    row = lax.broadcasted_iota(jnp.int32, (CHUNK, CHUNK), 0)
    col = lax.broadcasted_iota(jnp.int32, (CHUNK, CHUNK), 1)
    incl = col <= row
    same_half = (row // half) == (col // half)
    diag_blocks = same_half & (col < row)
    off_block = (row >= half) & (col < half)

    for g0 in range(0, DN_HEADS, group):
        heads = list(range(g0, g0 + group))
        pairs = [(heads[i], heads[i + 1]) for i in range(0, group, 2)]

        hd = {}
        for h in heads:
            q = qkv_ref[:, h * HEAD_DIM:(h + 1) * HEAD_DIM]
            k = qkv_ref[:, DN_WIDTH + h * HEAD_DIM:DN_WIDTH + (h + 1) * HEAD_DIM]
            v = qkv_ref[:, 2 * DN_WIDTH + h * HEAD_DIM:2 * DN_WIDTH + (h + 1) * HEAD_DIM]
            beta = jnp.broadcast_to(beta_all[:, h:h + 1], (blk, HEAD_DIM))
            gc = jnp.broadcast_to(gc_all[:, DN_HEADS + h:DN_HEADS + h + 1], (blk, HEAD_DIM))
            eg = jnp.exp(gc)
            kbn = -(k * beta)
            hd[h] = dict(k=k, gc=gc, kb=_bf(k), lhs=(_bf(kbn), _bf(q)),
                         rhs=_bf(jnp.concatenate([v * beta, kbn * eg], axis=1)),
                         qeg=_bf(q * eg), gc_row=gc_t[DN_HEADS + h:DN_HEADS + h + 1, :])

        ln_bd, ln_off, qk, k_dec = {}, {}, {}, {}
        for c in range(nc):
            sl = slice(c * CHUNK, (c + 1) * CHUNK)
            for pr in pairs:
                lhs = _cat1(*[jnp.concatenate([hd[h]["lhs"][0][sl], hd[h]["lhs"][1][sl]], axis=0) for h in pr])
                a2 = _dot_nt(lhs, _bd(hd[pr[0]]["kb"][sl], hd[pr[1]]["kb"][sl]))
                for i, h in enumerate(pr):
                    d = hd[h]
                    a = a2[:, i * CHUNK:(i + 1) * CHUNK]
                    gcc = d["gc"][sl]
                    decay = jnp.exp(jnp.where(incl, gcc - d["gc_row"][:, sl], -jnp.inf))
                    ln = a[:CHUNK] * decay
                    ln_bd[h, c] = jnp.where(diag_blocks, ln, 0.0)
                    ln_off[h, c] = jnp.where(off_block, ln, 0.0)
                    qk[h, c] = _bf(a[CHUNK:] * decay)
                    g_last = gcc[CHUNK - 1:CHUNK, :]
                    k_dec[h, c] = _bf(d["k"][sl] * jnp.exp(g_last - gcc))

        units = [(pr, c) for c in range(nc) for pr in pairs]

        def pair_dot(xs, ws):
            y = _dot(_cat1(*xs), _bd(*ws))
            n = ws[0].shape[1]
            return y[:, :n], y[:, n:]

        p = dict(ln_bd)
        m = {}
        for (pr, c) in units:
            bs = [_bf(ln_bd[h, c]) for h in pr]
            for h, y in zip(pr, pair_dot(bs, bs)):
                m[h, c] = y
        for _ in range(int(math.log2(half)) - 2):
            for (pr, c) in units:
                mbs = [_bf(m[h, c]) for h in pr]
                xs = [jnp.concatenate([mb, _bf(p[h, c])], axis=0) for mb, h in zip(mbs, pr)]
                for h, x2 in zip(pr, pair_dot(xs, mbs)):
                    p[h, c] = p[h, c] + m[h, c] + x2[CHUNK:]
                    m[h, c] = x2[:CHUNK]
        for (pr, c) in units:
            ys = pair_dot([_bf(p[h, c]) for h in pr], [_bf(m[h, c]) for h in pr])
            for h, y in zip(pr, ys):
                p[h, c] = p[h, c] + m[h, c] + y
        for (pr, c) in units:
            pbs = [_bf(p[h, c]) for h in pr]
            ys = pair_dot([_bf(ln_off[h, c]) for h in pr], pbs)
            ys = [ln_off[h, c] + y for h, y in zip(pr, ys)]
            zs = pair_dot(pbs, [_bf(y) for y in ys])
            for h, y, z in zip(pr, ys, zs):
                p[h, c] = p[h, c] + y + z
        sol = {}
        for c in range(nc):
            for h in heads:
                rhs = hd[h]["rhs"][c * CHUNK:(c + 1) * CHUNK]
                sol[h, c] = rhs.astype(F32) + _dot(_bf(p[h, c]), rhs)

        states = {h: state_ref[h] for h in heads}
        outs = {h: [] for h in heads}
        for c in range(nc):
            sl = slice(c * CHUNK, (c + 1) * CHUNK)
            for pr in pairs:
                xs = [jnp.concatenate([_bf(sol[h, c][:, HEAD_DIM:]), hd[h]["qeg"][sl]], axis=0) for h in pr]
                wss = pair_dot(xs, [_bf(states[h]) for h in pr])
                v_new = [_bf(sol[h, c][:, :HEAD_DIM] + ws[:CHUNK]) for h, ws in zip(pr, wss)]
                intra = pair_dot([qk[h, c] for h in pr], v_new)
                for h, ws, vn, it in zip(pr, wss, v_new, intra):
                    outs[h].append(ws[CHUNK:] + it)
                    g_last = hd[h]["gc"][(c + 1) * CHUNK - 1:(c + 1) * CHUNK, :]
                    states[h] = states[h] * jnp.exp(g_last) + _dot_tn(k_dec[h, c], vn)
        for h in heads:
            state_ref[h] = states[h]
            o = jnp.concatenate(outs[h], axis=0) if nc > 1 else outs[h][0]
            cols = slice(h * HEAD_DIM, (h + 1) * HEAD_DIM)
            o_ref[:, cols] = _head_norm_gate(o, gain_ref[...], gate_ref[:, cols])


def _deltanet(qkv, rest, alog_row, dtb_row, gain, *, layer, batch, seq, blk=256, group=4):
    blk = min(blk, seq)
    nblk = seq // blk
    t = batch * seq
    row = lambda b, s: (b * nblk + s)
    return pl.pallas_call(
        functools.partial(_dn_kernel, blk=blk, group=group),
        grid=(batch, nblk),
        in_specs=[
            pl.BlockSpec((blk, QKV_WIDTH), lambda b, s: (row(b, s), 0)),
            pl.BlockSpec((blk, DN_WIDTH), lambda b, s: (row(b, s), REST_GATE_BLOCK)),
            pl.BlockSpec((blk, LANES), lambda b, s: (row(b, s), REST_SMALL_BLOCK)),
            pl.BlockSpec((None, 1, LANES), lambda b, s: (layer, 0, 0)),
            pl.BlockSpec((None, 1, LANES), lambda b, s: (layer, 0, 0)),
            pl.BlockSpec((None, 1, HEAD_DIM), lambda b, s: (layer, 0, 0)),
        ],
        out_specs=pl.BlockSpec((blk, DN_WIDTH), lambda b, s: (row(b, s), 0)),
        out_shape=jax.ShapeDtypeStruct((t, DN_WIDTH), BF16),
        scratch_shapes=[pltpu.VMEM((DN_HEADS, HEAD_DIM, HEAD_DIM), F32)],
        compiler_params=_params("parallel", "arbitrary"),
        name="deltanet",
    )(qkv, rest, rest, alog_row, dtb_row, gain)


def _hg_level_matrix():
    r = lax.broadcasted_iota(jnp.int32, (CHUNK, CHUNK), 0)
    c = lax.broadcasted_iota(jnp.int32, (CHUNK, CHUNK), 1)
    mats = [c <= r]
    for m in HG_LEVELS:
        mats.append(c <= (r // (2 * m)) * (2 * m) + m - 1)
    mat = jnp.concatenate([jnp.where(x, 1.0, 0.0).astype(BF16) for x in mats], axis=0)
    return _cat1(mat, mat)


def _hg_kernel(x_ref, lbt_ref, gain_ref, o_ref, state_ref, *, blk, layer):
    s = pl.program_id(1)

    @pl.when(s == 0)
    def _():
        state_ref[...] = jnp.zeros_like(state_ref)

    tbl = lbt_ref[...]
    e = jnp.exp(tbl - jnp.max(tbl, axis=0, keepdims=True))
    sm = e / jnp.sum(e, axis=0, keepdims=True)
    lb = jnp.zeros((1, HG_WIDTH), F32)
    for l in range(1, layer + 1):
        lb = lb + sm[l:l + 1, :]

    level_mat = _hg_level_matrix()
    row = lax.broadcasted_iota(jnp.int32, (CHUNK, CHUNK), 0)
    col = lax.broadcasted_iota(jnp.int32, (CHUNK, CHUNK), 1)
    rowv = lax.broadcasted_iota(jnp.int32, (CHUNK, HEAD_DIM), 0)
    eye = row == col
    pair_mask = [((row // (2 * m)) == (col // (2 * m))) & ((row // m) % 2 == 1) & ((col // m) % 2 == 0)
                 for m in HG_LEVELS]
    lower_rows = [(rowv // m) % 2 == 1 for m in HG_LEVELS]
    pairs = [(h, h + 1) for h in range(0, HG_HEADS, 2)]

    for c in range(blk // CHUNK):
        sl = slice(c * CHUNK, (c + 1) * CHUNK)
        z = x_ref[sl, HG_WIDTH:2 * HG_WIDTH]
        sig = _sigmoid(z)
        log2_f = jnp.log(lb + (1.0 - lb) * sig) * LOG2_E
        hk_all = (1.0 - lb) * (1.0 - sig)
        lf_hi, lf_lo = _split_bf16(log2_f)
        cums = _dot(level_mat, jnp.concatenate([lf_hi, lf_lo], axis=0))
        for pr in pairs:
            q, k, v, b, state_t = {}, {}, {}, {}, {}
            for h in pr:
                cols = slice(h * HEAD_DIM, (h + 1) * HEAD_DIM)
                q[h] = _silu(x_ref[sl, cols]) * (HEAD_DIM ** -0.5)
                k[h] = hk_all[:, cols]
                v[h] = _bf(x_ref[sl, 2 * HG_WIDTH + h * HEAD_DIM:2 * HG_WIDTH + (h + 1) * HEAD_DIM])
                b[h] = cums[0:CHUNK, cols]
                state_t[h] = state_ref[h]
            scores = [_dot_nt(_cat1(*[_bf(q[h]) for h in pr]), _bd(*[_bf(k[h]) for h in pr]))]
            for li in range(len(HG_LEVELS)):
                xs = []
                for h in pr:
                    ref = cums[(li + 1) * CHUNK:(li + 2) * CHUNK, h * HEAD_DIM:(h + 1) * HEAD_DIM]
                    scale = jnp.exp2(-jnp.abs(b[h] - ref))
                    xs.append(_bf(jnp.where(lower_rows[li], q[h], k[h]) * scale))
                scores.append(_dot_nt(_cat1(*xs), _bd(*xs)))
            att = []
            for i, h in enumerate(pr):
                a = jnp.zeros((CHUNK, CHUNK), F32)
                for li in range(len(HG_LEVELS)):
                    a = jnp.where(pair_mask[li], scores[li + 1][:, i * CHUNK:(i + 1) * CHUNK], a)
                att.append(_bf(jnp.where(eye, scores[0][:, i * CHUNK:(i + 1) * CHUNK], a)))
            intra = _dot(_cat1(*att), _bd(*[v[h] for h in pr]))
            inter = _dot_nt(_cat1(*[_bf(q[h] * jnp.exp2(b[h])) for h in pr]),
                            _bd(*[_bf(state_t[h]) for h in pr]))
            o2 = intra + inter
            for i, h in enumerate(pr):
                b_last = b[h][CHUNK - 1:CHUNK, :]
                k_dec = k[h] * jnp.exp2(b_last - b[h])
                state_ref[h] = state_t[h] * jnp.exp2(b_last) + _dot_tn(v[h], _bf(k_dec))
                gate = x_ref[sl, 3 * HG_WIDTH + h * HEAD_DIM:3 * HG_WIDTH + (h + 1) * HEAD_DIM]
                o_ref[sl, h * HEAD_DIM:(h + 1) * HEAD_DIM] = _head_norm_gate(
                    o2[:, i * HEAD_DIM:(i + 1) * HEAD_DIM], gain_ref[...], gate)


def _hgrn2(rest, lb_table, gain, *, layer, batch, seq, blk=256):
    blk = min(blk, seq)
    nblk = seq // blk
    t = batch * seq
    return pl.pallas_call(
        functools.partial(_hg_kernel, blk=blk, layer=layer),
        grid=(batch, nblk),
        in_specs=[
            pl.BlockSpec((blk, 4 * HG_WIDTH), lambda b, s: (b * nblk + s, 0)),
            pl.BlockSpec((DEPTH, HG_WIDTH), lambda b, s: (0, 0)),
            pl.BlockSpec((None, 1, HEAD_DIM), lambda b, s: (layer, 0, 0)),
        ],
        out_specs=pl.BlockSpec((blk, HG_WIDTH), lambda b, s: (b * nblk + s, 0)),
        out_shape=jax.ShapeDtypeStruct((t, HG_WIDTH), BF16),
        scratch_shapes=[pltpu.VMEM((HG_HEADS, HEAD_DIM, HEAD_DIM), F32)],
        compiler_params=_params("parallel", "arbitrary"),
        name="hgrn2",
    )(rest, lb_table, gain)


def _rope_kernel(cos_ref, sin_ref):
    shape = cos_ref.shape
    half = HEAD_DIM // 2
    pos = lax.broadcasted_iota(jnp.int32, shape, 0).astype(F32)
    lane = lax.broadcasted_iota(jnp.int32, shape, 1)
    inv_freq = jnp.exp((lane % half).astype(F32) * (-math.log(ROPE_THETA) / half))
    ang = pos * inv_freq
    cos_ref[...] = jnp.cos(ang)
    sin = jnp.sin(ang)
    sin_ref[...] = jnp.where(lane < half, -sin, sin)


def _rope_tables(seq):
    shape = jax.ShapeDtypeStruct((seq, HEAD_DIM), F32)
    return pl.pallas_call(_rope_kernel, out_shape=(shape, shape), name="rope_tables")()


def _rt_kernel(x_ref, cos_ref, sin_ref, gain_ref, o_ref, state_ref, *, blk):
    s = pl.program_id(1)

    @pl.when(s == 0)
    def _():
        state_ref[...] = jnp.zeros_like(state_ref)

    cos = cos_ref[...]
    sin = sin_ref[...]
    row = lax.broadcasted_iota(jnp.int32, (blk, blk), 0)
    col = lax.broadcasted_iota(jnp.int32, (blk, blk), 1)
    dist = (row - col).astype(F32)
    pos = lax.broadcasted_iota(jnp.int32, (blk, HEAD_DIM), 0).astype(F32)

    def rotary(x):
        return x * cos + pltpu.roll(x, HEAD_DIM // 2, axis=1) * sin

    for h in range(RT_HEADS):
        log_gamma = math.log(1.0 - 2.0 ** (-5.0 - h))
        cols = slice(h * HEAD_DIM, (h + 1) * HEAD_DIM)
        q = rotary(x_ref[:, cols]) * (HEAD_DIM ** -0.5)
        k = rotary(x_ref[:, RT_WIDTH + h * HEAD_DIM:RT_WIDTH + (h + 1) * HEAD_DIM])
        v = x_ref[:, 2 * RT_WIDTH + h * HEAD_DIM:2 * RT_WIDTH + (h + 1) * HEAD_DIM]
        vb = _bf(v)
        dmat = jnp.exp(jnp.where(col <= row, dist * log_gamma, -jnp.inf))
        att = _dot_nt(_bf(q), _bf(k)) * dmat
        state = state_ref[h]
        o = _dot(_bf(att), vb) + _dot(_bf(q * jnp.exp((pos + 1.0) * log_gamma)), _bf(state))
        k_dec = k * jnp.exp((blk - 1.0 - pos) * log_gamma)
        state_ref[h] = state * math.exp(blk * log_gamma) + _dot_tn(_bf(k_dec), vb)
        gate = x_ref[:, 3 * RT_WIDTH + h * HEAD_DIM:3 * RT_WIDTH + (h + 1) * HEAD_DIM]
        o_ref[:, cols] = _head_norm_gate(o, gain_ref[...], gate)


def _retention(rest, cos_tab, sin_tab, gain, *, layer, batch, seq, blk=256):
    blk = min(blk, seq)
    nblk = seq // blk
    t = batch * seq
    return pl.pallas_call(
        functools.partial(_rt_kernel, blk=blk),
        grid=(batch, nblk),
        in_specs=[
            pl.BlockSpec((blk, 4 * RT_WIDTH), lambda b, s: (b * nblk + s, 1)),
            pl.BlockSpec((blk, HEAD_DIM), lambda b, s: (s, 0)),
            pl.BlockSpec((blk, HEAD_DIM), lambda b, s: (s, 0)),
            pl.BlockSpec((None, 1, HEAD_DIM), lambda b, s: (layer, 0, 0)),
        ],
        out_specs=pl.BlockSpec((blk, RT_WIDTH), lambda b, s: (b * nblk + s, 0)),
        out_shape=jax.ShapeDtypeStruct((t, RT_WIDTH), BF16),
        scratch_shapes=[pltpu.VMEM((RT_HEADS, HEAD_DIM, HEAD_DIM), F32)],
        compiler_params=_params("parallel", "arbitrary"),
        name="retention",
    )(rest, cos_tab, sin_tab, gain)


def _outproj_kernel(h_ref, dn_ref, hg_ref, rt_ref, w_ref, o_ref):
    acc = _dot(dn_ref[...], w_ref[0:DN_WIDTH, :])
    acc += _dot(hg_ref[...], w_ref[DN_WIDTH:DN_WIDTH + HG_WIDTH, :])
    acc += _dot(rt_ref[...], w_ref[DN_WIDTH + HG_WIDTH:, :])
    o_ref[...] = h_ref[...] + acc


def _outproj(h, o_dn, o_hg, o_rt, w, *, layer, tm=512):
    t, d = h.shape
    tm = min(tm, t)
    return pl.pallas_call(
        _outproj_kernel,
        grid=(t // tm,),
        in_specs=[
            pl.BlockSpec((tm, d), lambda i: (i, 0)),
            pl.BlockSpec((tm, DN_WIDTH), lambda i: (i, 0)),
            pl.BlockSpec((tm, HG_WIDTH), lambda i: (i, 0)),
            pl.BlockSpec((tm, RT_WIDTH), lambda i: (i, 0)),
            pl.BlockSpec((None, d, d), lambda i: (layer, 0, 0)),
        ],
        out_specs=pl.BlockSpec((tm, d), lambda i: (i, 0)),
        out_shape=jax.ShapeDtypeStruct((t, d), F32),
        compiler_params=_params("parallel"),
        name="outproj",
    )(h, o_dn, o_hg, o_rt, w)


def _ple_kernel(h_ref, nw_ref, wg_ref, p_ref, wp_ref, fn_ref, o_ref, *, final):
    x = h_ref[...]
    gate = _sigmoid(_dot(_bf(_rms_norm(x, nw_ref[...])), wg_ref[...]))
    y = x + gate * _dot(_bf(p_ref[...]), wp_ref[...])
    if final:
        y = _rms_norm(y, fn_ref[...])
    o_ref[...] = y


def _ple(h, nw, wg, p, wp, fn, *, layer, final, tm=512):
    t, d = h.shape
    tm = min(tm, t)
    return pl.pallas_call(
        functools.partial(_ple_kernel, final=final),
        grid=(t // tm,),
        in_specs=[
            pl.BlockSpec((tm, d), lambda i: (i, 0)),
            pl.BlockSpec((None, 1, d), lambda i: (layer, 0, 0)),
            pl.BlockSpec((None, d, d), lambda i: (layer, 0, 0)),
            pl.BlockSpec((None, tm, PLE_DIM), lambda i: (layer, i, 0)),
            pl.BlockSpec((None, PLE_DIM, d), lambda i: (layer, 0, 0)),
            pl.BlockSpec((1, d), lambda i: (0, 0)),
        ],
        out_specs=pl.BlockSpec((tm, d), lambda i: (i, 0)),
        out_shape=jax.ShapeDtypeStruct((t, d), F32),
        compiler_params=_params("parallel"),
        name="ple",
    )(h, nw, wg, p, wp, fn)


def _rest_columns(w):
    c0 = QKV_WIDTH
    c1 = c0 + DN_WIDTH
    c2 = c1 + 2 * DN_HEADS
    pad = jnp.zeros(w.shape[:2] + (REST_WIDTH - REST_MAIN - 2 * DN_HEADS,), w.dtype)
    return jnp.concatenate([w[..., c2:], w[..., c0:c1], w[..., c1:c2], pad], axis=-1)


def _lane_rows(x, offset):
    n = x.shape[1]
    return jnp.pad(x.astype(F32), ((0, 0), (offset, LANES - offset - n)))[:, None, :]


def kernel(x, p, ffn1_norm, ffn1_w_gate_up, ffn1_w_down, mix_norm, w_in, dn_conv, dn_a_log, dn_dt_bias, dn_out_norm, hg_lower_bounds, hg_out_norm, rt_out_norm, w_out, ffn2_norm, ffn2_w_gate_up, ffn2_w_down, ple_norm, ple_w_gate, ple_w_proj, final_norm):
    batch, seq, d = x.shape
    t = batch * seq
    h = x.reshape(t, d)
    cos_tab, sin_tab = _rope_tables(seq)
    rows = lambda v: v.astype(F32)[:, None, :]
    ffn1_wgu, ffn1_wd = _bf(ffn1_w_gate_up), _bf(ffn1_w_down)
    ffn2_wgu, ffn2_wd = _bf(ffn2_w_gate_up), _bf(ffn2_w_down)
    w_in_bf = _bf(w_in)
    w_rest = _rest_columns(w_in_bf)
    w_out_bf, ple_wg, ple_wp = _bf(w_out), _bf(ple_w_gate), _bf(ple_w_proj)
    ffn1_nw, ffn2_nw, mix_nw, ple_nw = rows(ffn1_norm), rows(ffn2_norm), rows(mix_norm), rows(ple_norm)
    dn_gain, hg_gain, rt_gain = rows(dn_out_norm), rows(hg_out_norm), rows(rt_out_norm)
    alog_rows, dtb_rows = _lane_rows(dn_a_log, DN_HEADS), _lane_rows(dn_dt_bias, DN_HEADS)
    p_tok = p.reshape(DEPTH, t, PLE_DIM)
    final_nw = final_norm.astype(F32)[None, :]
    for i in range(DEPTH):
        h = _ffn(h, ffn1_nw, ffn1_wgu, ffn1_wd, layer=i)
        qkv = _qkvproj(h, mix_nw, w_in_bf, dn_conv, layer=i, seq=seq)
        rest = _restproj(h, mix_nw, w_rest, layer=i)
        o_dn = _deltanet(qkv, rest, alog_rows, dtb_rows, dn_gain, layer=i, batch=batch, seq=seq)
        o_hg = _hgrn2(rest, hg_lower_bounds, hg_gain, layer=i, batch=batch, seq=seq)
        o_rt = _retention(rest, cos_tab, sin_tab, rt_gain, layer=i, batch=batch, seq=seq)
        h = _outproj(h, o_dn, o_hg, o_rt, w_out_bf, layer=i)
        h = _ffn(h, ffn2_nw, ffn2_wgu, ffn2_wd, layer=i)
        h = _ple(h, ple_nw, ple_wg, p_tok, ple_wp, final_nw, layer=i, final=(i == DEPTH - 1))
    return h.reshape(batch, seq, d)
```

```python
import functools
import math

import jax
import jax.numpy as jnp
from jax import lax
from jax.experimental import pallas as pl
from jax.experimental.pallas import tpu as pltpu

F32 = jnp.float32
BF16 = jnp.bfloat16

D_MODEL = 2048
D_FF = 5632
DEPTH = 2
DN_HEADS = 8
HG_HEADS = 4
RT_HEADS = 4
HEAD_DIM = 128
DN_WIDTH = DN_HEADS * HEAD_DIM
HG_WIDTH = HG_HEADS * HEAD_DIM
RT_WIDTH = RT_HEADS * HEAD_DIM
QKV_WIDTH = 3 * DN_WIDTH
CONV_WIDTH = 4
PLE_DIM = 256
ROPE_THETA = 10000.0
NORM_EPS = 1e-6
LOG2_E = math.log2(math.e)

LANES = 128
SUBLANES = 8
VMEM_LIMIT = 56 * 1024 * 1024

REST_MAIN = 4 * HG_WIDTH + 4 * RT_WIDTH + DN_WIDTH
REST_WIDTH = REST_MAIN + 256
REST_GATE_BLOCK = (4 * HG_WIDTH + 4 * RT_WIDTH) // DN_WIDTH
REST_SMALL_BLOCK = REST_MAIN // LANES

CHUNK = 128
HG_LEVELS = (64, 32, 16, 8, 4, 2, 1)


def _dot(a, b):
    return jnp.dot(a, b, preferred_element_type=F32)


def _dot_nt(a, b):
    return lax.dot_general(a, b, (((1,), (1,)), ((), ())), preferred_element_type=F32)


def _dot_tn(a, b):
    return lax.dot_general(a, b, (((0,), (0,)), ((), ())), preferred_element_type=F32)


def _bf(x):
    return x.astype(BF16)


def _sigmoid(x):
    return 1.0 / (1.0 + jnp.exp(-x))


def _silu(x):
    return x * _sigmoid(x)


def _rms_norm(x, w):
    return x * lax.rsqrt(jnp.mean(x * x, axis=-1, keepdims=True) + NORM_EPS) * w


def _split_bf16(x):
    hi = x.astype(BF16)
    lo = (x - hi.astype(F32)).astype(BF16)
    return hi, lo


def _params(*sem):
    return pltpu.CompilerParams(dimension_semantics=sem, vmem_limit_bytes=VMEM_LIMIT)


def _ffn_kernel(h_ref, nw_ref, wg_ref, wu_ref, wd_ref, o_ref, hn_ref, acc_ref, *, nf):
    j = pl.program_id(1)

    @pl.when(j == 0)
    def _():
        hn_ref[...] = _bf(_rms_norm(h_ref[...], nw_ref[...]))
        acc_ref[...] = jnp.zeros_like(acc_ref)

    hn = hn_ref[...]
    g = _dot(hn, wg_ref[...])
    u = _dot(hn, wu_ref[...])
    acc_ref[...] += _dot(_bf(_silu(g) * u), wd_ref[...])

    @pl.when(j == nf - 1)
    def _():
        o_ref[...] = h_ref[...] + 0.5 * acc_ref[...]


def _ffn(h, nw, wgu, wd, *, layer, tm=1024, tf=256):
    t, d = h.shape
    tm = min(tm, t)
    nf = D_FF // tf
    return pl.pallas_call(
        functools.partial(_ffn_kernel, nf=nf),
        grid=(t // tm, nf),
        in_specs=[
            pl.BlockSpec((tm, d), lambda i, j: (i, 0)),
            pl.BlockSpec((None, 1, d), lambda i, j: (layer, 0, 0)),
            pl.BlockSpec((None, d, tf), lambda i, j: (layer, 0, j)),
            pl.BlockSpec((None, d, tf), lambda i, j: (layer, 0, j + nf)),
            pl.BlockSpec((None, tf, d), lambda i, j: (layer, j, 0)),
        ],
        out_specs=pl.BlockSpec((tm, d), lambda i, j: (i, 0)),
        out_shape=jax.ShapeDtypeStruct((t, d), F32),
        scratch_shapes=[pltpu.VMEM((tm, d), BF16), pltpu.VMEM((tm, d), F32)],
        compiler_params=_params("parallel", "arbitrary"),
        name="ffn",
    )(h, nw, wgu, wgu, wd)


def _qkv_kernel(h_ref, nw_ref, w_ref, cw_ref, o_ref, hn_ref, tail_ref, *, tiles_per_seq, cb):
    i = pl.program_id(0)
    first = (i % tiles_per_seq) == 0
    tm = h_ref.shape[0]
    hn_ref[...] = _bf(_rms_norm(h_ref[...], nw_ref[...]))
    nchunk = QKV_WIDTH // cb
    y_next = _dot(hn_ref[...], w_ref[:, 0:cb])
    for c in range(nchunk):
        cs = slice(c * cb, (c + 1) * cb)
        y = y_next
        if c + 1 < nchunk:
            y_next = _dot(hn_ref[...], w_ref[:, (c + 1) * cb:(c + 2) * cb])
        tail = jnp.where(first, 0.0, tail_ref[:, cs])
        tail_ref[:, cs] = y[tm - SUBLANES:, :]
        z = jnp.concatenate([tail, y], axis=0)
        acc = y * cw_ref[CONV_WIDTH - 1:CONV_WIDTH, cs]
        for j in range(CONV_WIDTH - 1):
            shifted = pltpu.roll(z, CONV_WIDTH - 1 - j, axis=0)[SUBLANES:, :]
            acc = acc + shifted * cw_ref[j:j + 1, cs]
        x = _silu(acc)
        for hh in range(cb // HEAD_DIM):
            col = c * cb + hh * HEAD_DIM
            xh = x[:, hh * HEAD_DIM:(hh + 1) * HEAD_DIM]
            if col < 2 * DN_WIDTH:
                r = lax.rsqrt(jnp.sum(xh * xh, axis=-1, keepdims=True) + NORM_EPS)
                if col < DN_WIDTH:
                    r = r * (HEAD_DIM ** -0.5)
                xh = xh * r
            o_ref[:, col:col + HEAD_DIM] = xh
        zero = jnp.minimum(jnp.abs(xh[0:2 * SUBLANES, :]), 0.0)
        hn_ref[0:2 * SUBLANES, 0:HEAD_DIM] += _bf(zero)


def _qkvproj(h, nw, w, conv_w, *, layer, seq, tm=512, cb=256):
    t, d = h.shape
    tm = min(tm, seq)
    return pl.pallas_call(
        functools.partial(_qkv_kernel, tiles_per_seq=seq // tm, cb=cb),
        grid=(t // tm,),
        in_specs=[
            pl.BlockSpec((tm, d), lambda i: (i, 0)),
            pl.BlockSpec((None, 1, d), lambda i: (layer, 0, 0)),
            pl.BlockSpec((None, d, QKV_WIDTH), lambda i: (layer, 0, 0)),
            pl.BlockSpec((None, CONV_WIDTH, QKV_WIDTH), lambda i: (layer, 0, 0)),
        ],
        out_specs=pl.BlockSpec((tm, QKV_WIDTH), lambda i: (i, 0)),
        out_shape=jax.ShapeDtypeStruct((t, QKV_WIDTH), F32),
        scratch_shapes=[pltpu.VMEM((tm, d), BF16), pltpu.VMEM((SUBLANES, QKV_WIDTH), F32)],
        compiler_params=_params("arbitrary"),
        name="qkvproj",
    )(h, nw, w, conv_w)


def _restproj_kernel(h_ref, nw_ref, w_ref, o_ref, hn_ref):
    @pl.when(pl.program_id(1) == 0)
    def _():
        hn_ref[...] = _bf(_rms_norm(h_ref[...], nw_ref[...]))

    o_ref[...] = _dot(hn_ref[...], w_ref[...])


def _restproj(h, nw, w, *, layer, tm=1024, tn=1792):
    t, d = h.shape
    tm = min(tm, t)
    n = w.shape[-1]
    return pl.pallas_call(
        _restproj_kernel,
        grid=(t // tm, n // tn),
        in_specs=[
            pl.BlockSpec((tm, d), lambda i, j: (i, 0)),
            pl.BlockSpec((None, 1, d), lambda i, j: (layer, 0, 0)),
            pl.BlockSpec((None, d, tn), lambda i, j: (layer, 0, j)),
        ],
        out_specs=pl.BlockSpec((tm, tn), lambda i, j: (i, j)),
        out_shape=jax.ShapeDtypeStruct((t, n), F32),
        scratch_shapes=[pltpu.VMEM((tm, d), BF16)],
        compiler_params=_params("parallel", "arbitrary"),
        name="restproj",
    )(h, nw, w)


def _chunk_tril_bf16(blk):
    r = lax.broadcasted_iota(jnp.int32, (blk, blk), 0)
    c = lax.broadcasted_iota(jnp.int32, (blk, blk), 1)
    return jnp.where((r // CHUNK == c // CHUNK) & (c <= r), 1.0, 0.0).astype(BF16)


def _cat1(*xs):
    return jnp.concatenate(xs, axis=1)


def _bd(a, b):
    z = jnp.zeros_like(a)
    return jnp.concatenate([_cat1(a, z), _cat1(z, b)], axis=0)


def _head_norm_gate(o, gain, gate):
    o = o * lax.rsqrt(jnp.mean(o * o, axis=-1, keepdims=True) + NORM_EPS) * gain
    return _bf(o * _silu(gate))


def _dn_kernel(qkv_ref, gate_ref, small_ref, alog_ref, dtb_ref, gain_ref, o_ref, state_ref, *, blk, group):
    s = pl.program_id(1)
    nc = blk // CHUNK
    half = CHUNK // 2

    @pl.when(s == 0)
    def _():
        state_ref[...] = jnp.zeros_like(state_ref)

    small = small_ref[...]
    beta_all = _sigmoid(small)
    x = small + dtb_ref[...]
    softplus = jnp.maximum(x, 0.0) + jnp.log(1.0 + jnp.exp(-jnp.abs(x)))
    g_all = -(jnp.exp(alog_ref[...]) * softplus)
    tril = _chunk_tril_bf16(blk)
    g_hi, g_lo = _split_bf16(g_all)
    gc_all = _dot(tril, g_hi) + _dot(tril, g_lo)
    gc_t = gc_all.T

    row = lax.broadcasted_iota(jnp.int32, (CHUNK, CHUNK), 0)
    col = lax.broadcasted_iota(jnp.int32, (CHUNK, CHUNK), 1)
    incl = col <= row
    same_half = (row // half) == (col // half)
    diag_blocks = same_half & (col < row)
    off_block = (row >= half) & (col < half)

    for g0 in range(0, DN_HEADS, group):
        heads = list(range(g0, g0 + group))
        pairs = [(heads[i], heads[i + 1]) for i in range(0, group, 2)]

        hd = {}
        for h in heads:
            q = qkv_ref[:, h * HEAD_DIM:(h + 1) * HEAD_DIM]
            k = qkv_ref[:, DN_WIDTH + h * HEAD_DIM:DN_WIDTH + (h + 1) * HEAD_DIM]
            v = qkv_ref[:, 2 * DN_WIDTH + h * HEAD_DIM:2 * DN_WIDTH + (h + 1) * HEAD_DIM]
            beta = jnp.broadcast_to(beta_all[:, h:h + 1], (blk, HEAD_DIM))
            gc = jnp.broadcast_to(gc_all[:, DN_HEADS + h:DN_HEADS + h + 1], (blk, HEAD_DIM))
            eg = jnp.exp(gc)
            kbn = -(k * beta)
            hd[h] = dict(k=k, gc=gc, kb=_bf(k), lhs=(_bf(kbn), _bf(q)),
                         rhs=_bf(jnp.concatenate([v * beta, kbn * eg], axis=1)),
                         qeg=_bf(q * eg), gc_row=gc_t[DN_HEADS + h:DN_HEADS + h + 1, :])

        ln_bd, ln_off, qk, k_dec = {}, {}, {}, {}
        for c in range(nc):
            sl = slice(c * CHUNK, (c + 1) * CHUNK)
            for pr in pairs:
                lhs = _cat1(*[jnp.concatenate([hd[h]["lhs"][0][sl], hd[h]["lhs"][1][sl]], axis=0) for h in pr])
                a2 = _dot_nt(lhs, _bd(hd[pr[0]]["kb"][sl], hd[pr[1]]["kb"][sl]))
                for i, h in enumerate(pr):
                    d = hd[h]
                    a = a2[:, i * CHUNK:(i + 1) * CHUNK]
                    gcc = d["gc"][sl]
                    decay = jnp.exp(jnp.where(incl, gcc - d["gc_row"][:, sl], -jnp.inf))
                    ln = a[:CHUNK] * decay
                    ln_bd[h, c] = jnp.where(diag_blocks, ln, 0.0)
                    ln_off[h, c] = jnp.where(off_block, ln, 0.0)
                    qk[h, c] = _bf(a[CHUNK:] * decay)
                    g_last = gcc[CHUNK - 1:CHUNK, :]
                    k_dec[h, c] = _bf(d["k"][sl] * jnp.exp(g_last - gcc))

        units = [(pr, c) for c in range(nc) for pr in pairs]

        def pair_dot(xs, ws):
            y = _dot(_cat1(*xs), _bd(*ws))
            n = ws[0].shape[1]
            return y[:, :n], y[:, n:]

        p = dict(ln_bd)
        m = {}
        for (pr, c) in units:
            bs = [_bf(ln_bd[h, c]) for h in pr]
            for h, y in zip(pr, pair_dot(bs, bs)):
                m[h, c] = y
        for _ in range(int(math.log2(half)) - 2):
            for (pr, c) in units:
                mbs = [_bf(m[h, c]) for h in pr]
                xs = [jnp.concatenate([mb, _bf(p[h, c])], axis=0) for mb, h in zip(mbs, pr)]
                for h, x2 in zip(pr, pair_dot(xs, mbs)):
                    p[h, c] = p[h, c] + m[h, c] + x2[CHUNK:]
                    m[h, c] = x2[:CHUNK]
        for (pr, c) in units:
            ys = pair_dot([_bf(p[h, c]) for h in pr], [_bf(m[h, c]) for h in pr])
            for h, y in zip(pr, ys):
                p[h, c] = p[h, c] + m[h, c] + y
        for (pr, c) in units:
            pbs = [_bf(p[h, c]) for h in pr]
            ys = pair_dot([_bf(ln_off[h, c]) for h in pr], pbs)
            ys = [ln_off[h, c] + y for h, y in zip(pr, ys)]
            zs = pair_dot(pbs, [_bf(y) for y in ys])
            for h, y, z in zip(pr, ys, zs):
                p[h, c] = p[h, c] + y + z
        sol = {}
        for c in range(nc):
            for h in heads:
                rhs = hd[h]["rhs"][c * CHUNK:(c + 1) * CHUNK]
                sol[h, c] = rhs.astype(F32) + _dot(_bf(p[h, c]), rhs)

        states = {h: state_ref[h] for h in heads}
        outs = {h: [] for h in heads}
        for c in range(nc):
            sl = slice(c * CHUNK, (c + 1) * CHUNK)
            for pr in pairs:
                xs = [jnp.concatenate([_bf(sol[h, c][:, HEAD_DIM:]), hd[h]["qeg"][sl]], axis=0) for h in pr]
                wss = pair_dot(xs, [_bf(states[h]) for h in pr])
                v_new = [_bf(sol[h, c][:, :HEAD_DIM] + ws[:CHUNK]) for h, ws in zip(pr, wss)]
                intra = pair_dot([qk[h, c] for h in pr], v_new)
                for h, ws, vn, it in zip(pr, wss, v_new, intra):
                    outs[h].append(ws[CHUNK:] + it)
                    g_last = hd[h]["gc"][(c + 1) * CHUNK - 1:(c + 1) * CHUNK, :]
                    states[h] = states[h] * jnp.exp(g_last) + _dot_tn(k_dec[h, c], vn)
        for h in heads:
            state_ref[h] = states[h]
            o = jnp.concatenate(outs[h], axis=0) if nc > 1 else outs[h][0]
            cols = slice(h * HEAD_DIM, (h + 1) * HEAD_DIM)
            o_ref[:, cols] = _head_norm_gate(o, gain_ref[...], gate_ref[:, cols])


def _deltanet(qkv, rest, alog_row, dtb_row, gain, *, layer, batch, seq, blk=512, group=4):
    blk = min(blk, seq)
    nblk = seq // blk
    t = batch * seq
    row = lambda b, s: (b * nblk + s)
    return pl.pallas_call(
        functools.partial(_dn_kernel, blk=blk, group=group),
        grid=(batch, nblk),
        in_specs=[
            pl.BlockSpec((blk, QKV_WIDTH), lambda b, s: (row(b, s), 0)),
            pl.BlockSpec((blk, DN_WIDTH), lambda b, s: (row(b, s), REST_GATE_BLOCK)),
            pl.BlockSpec((blk, LANES), lambda b, s: (row(b, s), REST_SMALL_BLOCK)),
            pl.BlockSpec((None, 1, LANES), lambda b, s: (layer, 0, 0)),
            pl.BlockSpec((None, 1, LANES), lambda b, s: (layer, 0, 0)),
            pl.BlockSpec((None, 1, HEAD_DIM), lambda b, s: (layer, 0, 0)),
        ],
        out_specs=pl.BlockSpec((blk, DN_WIDTH), lambda b, s: (row(b, s), 0)),
        out_shape=jax.ShapeDtypeStruct((t, DN_WIDTH), BF16),
        scratch_shapes=[pltpu.VMEM((DN_HEADS, HEAD_DIM, HEAD_DIM), F32)],
        compiler_params=_params("parallel", "arbitrary"),
        name="deltanet",
    )(qkv, rest, rest, alog_row, dtb_row, gain)


def _hg_level_matrix():
    r = lax.broadcasted_iota(jnp.int32, (CHUNK, CHUNK), 0)
    c = lax.broadcasted_iota(jnp.int32, (CHUNK, CHUNK), 1)
    mats = [c <= r]
    for m in HG_LEVELS:
        mats.append(c <= (r // (2 * m)) * (2 * m) + m - 1)
    mat = jnp.concatenate([jnp.where(x, 1.0, 0.0).astype(BF16) for x in mats], axis=0)
    return _cat1(mat, mat)


def _hg_kernel(x_ref, lbt_ref, gain_ref, o_ref, state_ref, *, blk, layer):
    s = pl.program_id(1)

    @pl.when(s == 0)
    def _():
        state_ref[...] = jnp.zeros_like(state_ref)

    tbl = lbt_ref[...]
    e = jnp.exp(tbl - jnp.max(tbl, axis=0, keepdims=True))
    sm = e / jnp.sum(e, axis=0, keepdims=True)
    lb = jnp.zeros((1, HG_WIDTH), F32)
    for l in range(1, layer + 1):
        lb = lb + sm[l:l + 1, :]

    level_mat = _hg_level_matrix()
    row = lax.broadcasted_iota(jnp.int32, (CHUNK, CHUNK), 0)
    col = lax.broadcasted_iota(jnp.int32, (CHUNK, CHUNK), 1)
    rowv = lax.broadcasted_iota(jnp.int32, (CHUNK, HEAD_DIM), 0)
    eye = row == col
    pair_mask = [((row // (2 * m)) == (col // (2 * m))) & ((row // m) % 2 == 1) & ((col // m) % 2 == 0)
                 for m in HG_LEVELS]
    lower_rows = [(rowv // m) % 2 == 1 for m in HG_LEVELS]
    pairs = [(h, h + 1) for h in range(0, HG_HEADS, 2)]

    for c in range(blk // CHUNK):
        sl = slice(c * CHUNK, (c + 1) * CHUNK)
        z = x_ref[sl, HG_WIDTH:2 * HG_WIDTH]
        sig = _sigmoid(z)
        log2_f = jnp.log(lb + (1.0 - lb) * sig) * LOG2_E
        hk_all = (1.0 - lb) * (1.0 - sig)
        lf_hi, lf_lo = _split_bf16(log2_f)
        cums = _dot(level_mat, jnp.concatenate([lf_hi, lf_lo], axis=0))
        for pr in pairs:
            q, k, v, b, state_t = {}, {}, {}, {}, {}
            for h in pr:
                cols = slice(h * HEAD_DIM, (h + 1) * HEAD_DIM)
                q[h] = _silu(x_ref[sl, cols]) * (HEAD_DIM ** -0.5)
                k[h] = hk_all[:, cols]
                v[h] = _bf(x_ref[sl, 2 * HG_WIDTH + h * HEAD_DIM:2 * HG_WIDTH + (h + 1) * HEAD_DIM])
                b[h] = cums[0:CHUNK, cols]
                state_t[h] = state_ref[h]
            scores = [_dot_nt(_cat1(*[_bf(q[h]) for h in pr]), _bd(*[_bf(k[h]) for h in pr]))]
            for li in range(len(HG_LEVELS)):
                xs = []
                for h in pr:
                    ref = cums[(li + 1) * CHUNK:(li + 2) * CHUNK, h * HEAD_DIM:(h + 1) * HEAD_DIM]
                    scale = jnp.exp2(-jnp.abs(b[h] - ref))
                    xs.append(_bf(jnp.where(lower_rows[li], q[h], k[h]) * scale))
                scores.append(_dot_nt(_cat1(*xs), _bd(*xs)))
            att = []
            for i, h in enumerate(pr):
                a = jnp.zeros((CHUNK, CHUNK), F32)
                for li in range(len(HG_LEVELS)):
                    a = jnp.where(pair_mask[li], scores[li + 1][:, i * CHUNK:(i + 1) * CHUNK], a)
                att.append(_bf(jnp.where(eye, scores[0][:, i * CHUNK:(i + 1) * CHUNK], a)))
            intra = _dot(_cat1(*att), _bd(*[v[h] for h in pr]))
            inter = _dot_nt(_cat1(*[_bf(q[h] * jnp.exp2(b[h])) for h in pr]),
                            _bd(*[_bf(state_t[h]) for h in pr]))
            o2 = intra + inter
            for i, h in enumerate(pr):
                b_last = b[h][CHUNK - 1:CHUNK, :]
                k_dec = k[h] * jnp.exp2(b_last - b[h])
                state_ref[h] = state_t[h] * jnp.exp2(b_last) + _dot_tn(v[h], _bf(k_dec))
                gate = x_ref[sl, 3 * HG_WIDTH + h * HEAD_DIM:3 * HG_WIDTH + (h + 1) * HEAD_DIM]
                o_ref[sl, h * HEAD_DIM:(h + 1) * HEAD_DIM] = _head_norm_gate(
                    o2[:, i * HEAD_DIM:(i + 1) * HEAD_DIM], gain_ref[...], gate)


def _hgrn2(rest, lb_table, gain, *, layer, batch, seq, blk=512):
    blk = min(blk, seq)
    nblk = seq // blk
    t = batch * seq
    return pl.pallas_call(
        functools.partial(_hg_kernel, blk=blk, layer=layer),
        grid=(batch, nblk),
        in_specs=[
            pl.BlockSpec((blk, 4 * HG_WIDTH), lambda b, s: (b * nblk + s, 0)),
            pl.BlockSpec((DEPTH, HG_WIDTH), lambda b, s: (0, 0)),
            pl.BlockSpec((None, 1, HEAD_DIM), lambda b, s: (layer, 0, 0)),
        ],
        out_specs=pl.BlockSpec((blk, HG_WIDTH), lambda b, s: (b * nblk + s, 0)),
        out_shape=jax.ShapeDtypeStruct((t, HG_WIDTH), BF16),
        scratch_shapes=[pltpu.VMEM((HG_HEADS, HEAD_DIM, HEAD_DIM), F32)],
        compiler_params=_params("parallel", "arbitrary"),
        name="hgrn2",
    )(rest, lb_table, gain)


def _rope_kernel(cos_ref, sin_ref):
    shape = cos_ref.shape
    half = HEAD_DIM // 2
    pos = lax.broadcasted_iota(jnp.int32, shape, 0).astype(F32)
    lane = lax.broadcasted_iota(jnp.int32, shape, 1)
    inv_freq = jnp.exp((lane % half).astype(F32) * (-math.log(ROPE_THETA) / half))
    ang = pos * inv_freq
    cos_ref[...] = jnp.cos(ang)
    sin = jnp.sin(ang)
    sin_ref[...] = jnp.where(lane < half, -sin, sin)


def _rope_tables(seq):
    shape = jax.ShapeDtypeStruct((seq, HEAD_DIM), F32)
    return pl.pallas_call(_rope_kernel, out_shape=(shape, shape), name="rope_tables")()


def _rt_kernel(x_ref, cos_ref, sin_ref, gain_ref, o_ref, state_ref, *, blk):
    s = pl.program_id(1)

    @pl.when(s == 0)
    def _():
        state_ref[...] = jnp.zeros_like(state_ref)

    cos = cos_ref[...]
    sin = sin_ref[...]
    row = lax.broadcasted_iota(jnp.int32, (blk, blk), 0)
    col = lax.broadcasted_iota(jnp.int32, (blk, blk), 1)
    dist = (row - col).astype(F32)
    pos = lax.broadcasted_iota(jnp.int32, (blk, HEAD_DIM), 0).astype(F32)

    def rotary(x):
        return x * cos + pltpu.roll(x, HEAD_DIM // 2, axis=1) * sin

    for h in range(RT_HEADS):
        log_gamma = math.log(1.0 - 2.0 ** (-5.0 - h))
        cols = slice(h * HEAD_DIM, (h + 1) * HEAD_DIM)
        q = rotary(x_ref[:, cols]) * (HEAD_DIM ** -0.5)
        k = rotary(x_ref[:, RT_WIDTH + h * HEAD_DIM:RT_WIDTH + (h + 1) * HEAD_DIM])
        v = x_ref[:, 2 * RT_WIDTH + h * HEAD_DIM:2 * RT_WIDTH + (h + 1) * HEAD_DIM]
        vb = _bf(v)
        dmat = jnp.exp(jnp.where(col <= row, dist * log_gamma, -jnp.inf))
        att = _dot_nt(_bf(q), _bf(k)) * dmat
        state = state_ref[h]
        o = _dot(_bf(att), vb) + _dot(_bf(q * jnp.exp((pos + 1.0) * log_gamma)), _bf(state))
        k_dec = k * jnp.exp((blk - 1.0 - pos) * log_gamma)
        state_ref[h] = state * math.exp(blk * log_gamma) + _dot_tn(_bf(k_dec), vb)
        gate = x_ref[:, 3 * RT_WIDTH + h * HEAD_DIM:3 * RT_WIDTH + (h + 1) * HEAD_DIM]
        o_ref[:, cols] = _head_norm_gate(o, gain_ref[...], gate)


def _retention(rest, cos_tab, sin_tab, gain, *, layer, batch, seq, blk=256):
    blk = min(blk, seq)
    nblk = seq // blk
    t = batch * seq
    return pl.pallas_call(
        functools.partial(_rt_kernel, blk=blk),
        grid=(batch, nblk),
        in_specs=[
            pl.BlockSpec((blk, 4 * RT_WIDTH), lambda b, s: (b * nblk + s, 1)),
            pl.BlockSpec((blk, HEAD_DIM), lambda b, s: (s, 0)),
            pl.BlockSpec((blk, HEAD_DIM), lambda b, s: (s, 0)),
            pl.BlockSpec((None, 1, HEAD_DIM), lambda b, s: (layer, 0, 0)),
        ],
        out_specs=pl.BlockSpec((blk, RT_WIDTH), lambda b, s: (b * nblk + s, 0)),
        out_shape=jax.ShapeDtypeStruct((t, RT_WIDTH), BF16),
        scratch_shapes=[pltpu.VMEM((RT_HEADS, HEAD_DIM, HEAD_DIM), F32)],
        compiler_params=_params("parallel", "arbitrary"),
        name="retention",
    )(rest, cos_tab, sin_tab, gain)


def _outproj_kernel(h_ref, dn_ref, hg_ref, rt_ref, w_ref, o_ref):
    acc = _dot(dn_ref[...], w_ref[0:DN_WIDTH, :])
    acc += _dot(hg_ref[...], w_ref[DN_WIDTH:DN_WIDTH + HG_WIDTH, :])
    acc += _dot(rt_ref[...], w_ref[DN_WIDTH + HG_WIDTH:, :])
    o_ref[...] = h_ref[...] + acc


def _outproj(h, o_dn, o_hg, o_rt, w, *, layer, tm=512):
    t, d = h.shape
    tm = min(tm, t)
    return pl.pallas_call(
        _outproj_kernel,
        grid=(t // tm,),
        in_specs=[
            pl.BlockSpec((tm, d), lambda i: (i, 0)),
            pl.BlockSpec((tm, DN_WIDTH), lambda i: (i, 0)),
            pl.BlockSpec((tm, HG_WIDTH), lambda i: (i, 0)),
            pl.BlockSpec((tm, RT_WIDTH), lambda i: (i, 0)),
            pl.BlockSpec((None, d, d), lambda i: (layer, 0, 0)),
        ],
        out_specs=pl.BlockSpec((tm, d), lambda i: (i, 0)),
        out_shape=jax.ShapeDtypeStruct((t, d), F32),
        compiler_params=_params("parallel"),
        name="outproj",
    )(h, o_dn, o_hg, o_rt, w)


def _ple_kernel(h_ref, nw_ref, wg_ref, p_ref, wp_ref, fn_ref, o_ref, *, final):
    x = h_ref[...]
    gate = _sigmoid(_dot(_bf(_rms_norm(x, nw_ref[...])), wg_ref[...]))
    y = x + gate * _dot(_bf(p_ref[...]), wp_ref[...])
    if final:
        y = _rms_norm(y, fn_ref[...])
    o_ref[...] = y


def _ple(h, nw, wg, p, wp, fn, *, layer, final, tm=512):
    t, d = h.shape
    tm = min(tm, t)
    return pl.pallas_call(
        functools.partial(_ple_kernel, final=final),
        grid=(t // tm,),
        in_specs=[
            pl.BlockSpec((tm, d), lambda i: (i, 0)),
            pl.BlockSpec((None, 1, d), lambda i: (layer, 0, 0)),
            pl.BlockSpec((None, d, d), lambda i: (layer, 0, 0)),
            pl.BlockSpec((None, tm, PLE_DIM), lambda i: (layer, i, 0)),
            pl.BlockSpec((None, PLE_DIM, d), lambda i: (layer, 0, 0)),
            pl.BlockSpec((1, d), lambda i: (0, 0)),
        ],
        out_specs=pl.BlockSpec((tm, d), lambda i: (i, 0)),
        out_shape=jax.ShapeDtypeStruct((t, d), F32),
        compiler_params=_params("parallel"),
        name="ple",
    )(h, nw, wg, p, wp, fn)


def _rest_columns(w):
    c0 = QKV_WIDTH
    c1 = c0 + DN_WIDTH
    c2 = c1 + 2 * DN_HEADS
    pad = jnp.zeros(w.shape[:2] + (REST_WIDTH - REST_MAIN - 2 * DN_HEADS,), w.dtype)
    return jnp.concatenate([w[..., c2:], w[..., c0:c1], w[..., c1:c2], pad], axis=-1)


def _lane_rows(x, offset):
    n = x.shape[1]
    return jnp.pad(x.astype(F32), ((0, 0), (offset, LANES - offset - n)))[:, None, :]


def kernel(x, p, ffn1_norm, ffn1_w_gate_up, ffn1_w_down, mix_norm, w_in, dn_conv, dn_a_log, dn_dt_bias, dn_out_norm, hg_lower_bounds, hg_out_norm, rt_out_norm, w_out, ffn2_norm, ffn2_w_gate_up, ffn2_w_down, ple_norm, ple_w_gate, ple_w_proj, final_norm):
    batch, seq, d = x.shape
    t = batch * seq
    h = x.reshape(t, d)
    cos_tab, sin_tab = _rope_tables(seq)
    rows = lambda v: v.astype(F32)[:, None, :]
    ffn1_wgu, ffn1_wd = _bf(ffn1_w_gate_up), _bf(ffn1_w_down)
    ffn2_wgu, ffn2_wd = _bf(ffn2_w_gate_up), _bf(ffn2_w_down)
    w_in_bf = _bf(w_in)
    w_rest = _rest_columns(w_in_bf)
    w_out_bf, ple_wg, ple_wp = _bf(w_out), _bf(ple_w_gate), _bf(ple_w_proj)
    ffn1_nw, ffn2_nw, mix_nw, ple_nw = rows(ffn1_norm), rows(ffn2_norm), rows(mix_norm), rows(ple_norm)
    dn_gain, hg_gain, rt_gain = rows(dn_out_norm), rows(hg_out_norm), rows(rt_out_norm)
    alog_rows, dtb_rows = _lane_rows(dn_a_log, DN_HEADS), _lane_rows(dn_dt_bias, DN_HEADS)
    p_tok = p.reshape(DEPTH, t, PLE_DIM)
    final_nw = final_norm.astype(F32)[None, :]
    for i in range(DEPTH):
        h = _ffn(h, ffn1_nw, ffn1_wgu, ffn1_wd, layer=i)
        qkv = _qkvproj(h, mix_nw, w_in_bf, dn_conv, layer=i, seq=seq)
        rest = _restproj(h, mix_nw, w_rest, layer=i)
        o_dn = _deltanet(qkv, rest, alog_rows, dtb_rows, dn_gain, layer=i, batch=batch, seq=seq)
        o_hg = _hgrn2(rest, hg_lower_bounds, hg_gain, layer=i, batch=batch, seq=seq)
        o_rt = _retention(rest, cos_tab, sin_tab, rt_gain, layer=i, batch=batch, seq=seq)
        h = _outproj(h, o_dn, o_hg, o_rt, w_out_bf, layer=i)
        h = _ffn(h, ffn2_nw, ffn2_wgu, ffn2_wd, layer=i)
        h = _ple(h, ple_nw, ple_wg, p_tok, ple_wp, final_nw, layer=i, final=(i == DEPTH - 1))
    return h.reshape(batch, seq, d)
```

```python
import functools
import math

import jax
import jax.numpy as jnp
from jax import lax
from jax.experimental import pallas as pl
from jax.experimental.pallas import tpu as pltpu

F32 = jnp.float32
BF16 = jnp.bfloat16

D_MODEL = 2048
D_FF = 5632
DEPTH = 2
DN_HEADS = 8
HG_HEADS = 4
RT_HEADS = 4
HEAD_DIM = 128
DN_WIDTH = DN_HEADS * HEAD_DIM
HG_WIDTH = HG_HEADS * HEAD_DIM
RT_WIDTH = RT_HEADS * HEAD_DIM
QKV_WIDTH = 3 * DN_WIDTH
CONV_WIDTH = 4
PLE_DIM = 256
ROPE_THETA = 10000.0
NORM_EPS = 1e-6
LOG2_E = math.log2(math.e)

LANES = 128
SUBLANES = 8
VMEM_LIMIT = 56 * 1024 * 1024

REST_MAIN = 4 * HG_WIDTH + 4 * RT_WIDTH + DN_WIDTH
REST_WIDTH = REST_MAIN + 256
REST_GATE_BLOCK = (4 * HG_WIDTH + 4 * RT_WIDTH) // DN_WIDTH
REST_SMALL_BLOCK = REST_MAIN // LANES

CHUNK = 128
HG_LEVELS = (64, 32, 16, 8, 4, 2, 1)


def _dot(a, b):
    return jnp.dot(a, b, preferred_element_type=F32)


def _dot_nt(a, b):
    return lax.dot_general(a, b, (((1,), (1,)), ((), ())), preferred_element_type=F32)


def _dot_tn(a, b):
    return lax.dot_general(a, b, (((0,), (0,)), ((), ())), preferred_element_type=F32)


def _bf(x):
    return x.astype(BF16)


def _sigmoid(x):
    return 1.0 / (1.0 + jnp.exp(-x))


def _silu(x):
    return x * _sigmoid(x)


def _rms_norm(x, w):
    return x * lax.rsqrt(jnp.mean(x * x, axis=-1, keepdims=True) + NORM_EPS) * w


def _split_bf16(x):
    hi = x.astype(BF16)
    lo = (x - hi.astype(F32)).astype(BF16)
    return hi, lo


def _params(*sem):
    return pltpu.CompilerParams(dimension_semantics=sem, vmem_limit_bytes=VMEM_LIMIT)


def _ffn_kernel(h_ref, nw_ref, wg_ref, wu_ref, wd_ref, o_ref, hn_ref, *, nf):
    j = pl.program_id(1)

    def partial_sum():
        hn = hn_ref[...]
        g = _dot(hn, wg_ref[...])
        u = _dot(hn, wu_ref[...])
        return _dot(_bf(_silu(g) * u), wd_ref[...])

    @pl.when(j == 0)
    def _():
        hn_ref[...] = _bf(_rms_norm(h_ref[...], nw_ref[...]))
        o_ref[...] = partial_sum()

    @pl.when((j > 0) & (j < nf - 1))
    def _():
        o_ref[...] += partial_sum()

    @pl.when(j == nf - 1)
    def _():
        o_ref[...] = h_ref[...] + 0.5 * (o_ref[...] + partial_sum())


def _ffn(h, nw, wgu, wd, *, layer, tm=1024, tf=512):
    t, d = h.shape
    tm = min(tm, t)
    nf = D_FF // tf
    return pl.pallas_call(
        functools.partial(_ffn_kernel, nf=nf),
        grid=(t // tm, nf),
        in_specs=[
            pl.BlockSpec((tm, d), lambda i, j: (i, 0)),
            pl.BlockSpec((None, 1, d), lambda i, j: (layer, 0, 0)),
            pl.BlockSpec((None, d, tf), lambda i, j: (layer, 0, j)),
            pl.BlockSpec((None, d, tf), lambda i, j: (layer, 0, j + nf)),
            pl.BlockSpec((None, tf, d), lambda i, j: (layer, j, 0)),
        ],
        out_specs=pl.BlockSpec((tm, d), lambda i, j: (i, 0)),
        out_shape=jax.ShapeDtypeStruct((t, d), F32),
        scratch_shapes=[pltpu.VMEM((tm, d), BF16)],
        compiler_params=_params("parallel", "arbitrary"),
        name="ffn",
    )(h, nw, wgu, wgu, wd)


def _qkv_kernel(h_ref, nw_ref, w_ref, cw_ref, o_ref, hn_ref, tail_ref, *, tiles_per_seq, cb):
    i = pl.program_id(0)
    first = (i % tiles_per_seq) == 0
    tm = h_ref.shape[0]
    hn_ref[...] = _bf(_rms_norm(h_ref[...], nw_ref[...]))
    nchunk = QKV_WIDTH // cb
    y_next = _dot(hn_ref[...], w_ref[:, 0:cb])
    for c in range(nchunk):
        cs = slice(c * cb, (c + 1) * cb)
        y = y_next
        if c + 1 < nchunk:
            y_next = _dot(hn_ref[...], w_ref[:, (c + 1) * cb:(c + 2) * cb])
        tail = jnp.where(first, 0.0, tail_ref[:, cs])
        tail_ref[:, cs] = y[tm - SUBLANES:, :]
        z = jnp.concatenate([tail, y], axis=0)
        acc = y * cw_ref[CONV_WIDTH - 1:CONV_WIDTH, cs]
        for j in range(CONV_WIDTH - 1):
            shifted = pltpu.roll(z, CONV_WIDTH - 1 - j, axis=0)[SUBLANES:, :]
            acc = acc + shifted * cw_ref[j:j + 1, cs]
        x = _silu(acc)
        for hh in range(cb // HEAD_DIM):
            col = c * cb + hh * HEAD_DIM
            xh = x[:, hh * HEAD_DIM:(hh + 1) * HEAD_DIM]
            if col < 2 * DN_WIDTH:
                r = lax.rsqrt(jnp.sum(xh * xh, axis=-1, keepdims=True) + NORM_EPS)
                if col < DN_WIDTH:
                    r = r * (HEAD_DIM ** -0.5)
                xh = xh * r
            o_ref[:, col:col + HEAD_DIM] = xh
        zero = jnp.minimum(jnp.abs(xh[0:2 * SUBLANES, :]), 0.0)
        hn_ref[0:2 * SUBLANES, 0:HEAD_DIM] += _bf(zero)


def _qkvproj(h, nw, w, conv_w, *, layer, seq, tm=512, cb=256):
    t, d = h.shape
    tm = min(tm, seq)
    return pl.pallas_call(
        functools.partial(_qkv_kernel, tiles_per_seq=seq // tm, cb=cb),
        grid=(t // tm,),
        in_specs=[
            pl.BlockSpec((tm, d), lambda i: (i, 0)),
            pl.BlockSpec((None, 1, d), lambda i: (layer, 0, 0)),
            pl.BlockSpec((None, d, QKV_WIDTH), lambda i: (layer, 0, 0)),
            pl.BlockSpec((None, CONV_WIDTH, QKV_WIDTH), lambda i: (layer, 0, 0)),
        ],
        out_specs=pl.BlockSpec((tm, QKV_WIDTH), lambda i: (i, 0)),
        out_shape=jax.ShapeDtypeStruct((t, QKV_WIDTH), F32),
        scratch_shapes=[pltpu.VMEM((tm, d), BF16), pltpu.VMEM((SUBLANES, QKV_WIDTH), F32)],
        compiler_params=_params("arbitrary"),
        name="qkvproj",
    )(h, nw, w, conv_w)


def _restproj_kernel(h_ref, nw_ref, w_ref, o_ref, hn_ref):
    @pl.when(pl.program_id(1) == 0)
    def _():
        hn_ref[...] = _bf(_rms_norm(h_ref[...], nw_ref[...]))

    o_ref[...] = _dot(hn_ref[...], w_ref[...])


def _restproj(h, nw, w, *, layer, tm=1024, tn=1792):
    t, d = h.shape
    tm = min(tm, t)
    n = w.shape[-1]
    return pl.pallas_call(
        _restproj_kernel,
        grid=(t // tm, n // tn),
        in_specs=[
            pl.BlockSpec((tm, d), lambda i, j: (i, 0)),
            pl.BlockSpec((None, 1, d), lambda i, j: (layer, 0, 0)),
            pl.BlockSpec((None, d, tn), lambda i, j: (layer, 0, j)),
        ],
        out_specs=pl.BlockSpec((tm, tn), lambda i, j: (i, j)),
        out_shape=jax.ShapeDtypeStruct((t, n), F32),
        scratch_shapes=[pltpu.VMEM((tm, d), BF16)],
        compiler_params=_params("parallel", "arbitrary"),
        name="restproj",
    )(h, nw, w)


def _chunk_tril_bf16(blk):
    r = lax.broadcasted_iota(jnp.int32, (blk, blk), 0)
    c = lax.broadcasted_iota(jnp.int32, (blk, blk), 1)
    return jnp.where((r // CHUNK == c // CHUNK) & (c <= r), 1.0, 0.0).astype(BF16)


def _cat1(*xs):
    return jnp.concatenate(xs, axis=1)


def _bd(a, b):
    z = jnp.zeros_like(a)
    return jnp.concatenate([_cat1(a, z), _cat1(z, b)], axis=0)


def _head_norm_gate(o, gain, gate):
    o = o * lax.rsqrt(jnp.mean(o * o, axis=-1, keepdims=True) + NORM_EPS) * gain
    return _bf(o * _silu(gate))


def _dn_kernel(qkv_ref, gate_ref, small_ref, alog_ref, dtb_ref, gain_ref, o_ref, state_ref, *, blk, group):
    s = pl.program_id(1)
    nc = blk // CHUNK
    half = CHUNK // 2

    @pl.when(s == 0)
    def _():
        state_ref[...] = jnp.zeros_like(state_ref)

    small = small_ref[...]
    beta_all = _sigmoid(small)
    x = small + dtb_ref[...]
    softplus = jnp.maximum(x, 0.0) + jnp.log(1.0 + jnp.exp(-jnp.abs(x)))
    g_all = -(jnp.exp(alog_ref[...]) * softplus)
    tril = _chunk_tril_bf16(blk)
    g_hi, g_lo = _split_bf16(g_all)
    gc_all = _dot(tril, g_hi) + _dot(tril, g_lo)
    gc_t = gc_all.T

    row = lax.broadcasted_iota(jnp.int32, (CHUNK, CHUNK), 0)
    col = lax.broadcasted_iota(jnp.int32, (CHUNK, CHUNK), 1)
    incl = col <= row
    same_half = (row // half) == (col // half)
    diag_blocks = same_half & (col < row)
    off_block = (row >= half) & (col < half)

    for g0 in range(0, DN_HEADS, group):
        heads = list(range(g0, g0 + group))
        pairs = [(heads[i], heads[i + 1]) for i in range(0, group, 2)]

        hd = {}
        for h in heads:
            q = qkv_ref[:, h * HEAD_DIM:(h + 1) * HEAD_DIM]
            k = qkv_ref[:, DN_WIDTH + h * HEAD_DIM:DN_WIDTH + (h + 1) * HEAD_DIM]
            v = qkv_ref[:, 2 * DN_WIDTH + h * HEAD_DIM:2 * DN_WIDTH + (h + 1) * HEAD_DIM]
            beta = jnp.broadcast_to(beta_all[:, h:h + 1], (blk, HEAD_DIM))
            gc = jnp.broadcast_to(gc_all[:, DN_HEADS + h:DN_HEADS + h + 1], (blk, HEAD_DIM))
            eg = jnp.exp(gc)
            kbn = -(k * beta)
            hd[h] = dict(k=k, gc=gc, kb=_bf(k), lhs=(_bf(kbn), _bf(q)),
                         rhs=_bf(jnp.concatenate([v * beta, kbn * eg], axis=1)),
                         qeg=_bf(q * eg), gc_row=gc_t[DN_HEADS + h:DN_HEADS + h + 1, :])

        ln_bd, ln_off, qk, k_dec = {}, {}, {}, {}
        for c in range(nc):
            sl = slice(c * CHUNK, (c + 1) * CHUNK)
            for pr in pairs:
                lhs = _cat1(*[jnp.concatenate([hd[h]["lhs"][0][sl], hd[h]["lhs"][1][sl]], axis=0) for h in pr])
                a2 = _dot_nt(lhs, _bd(hd[pr[0]]["kb"][sl], hd[pr[1]]["kb"][sl]))
                for i, h in enumerate(pr):
                    d = hd[h]
                    a = a2[:, i * CHUNK:(i + 1) * CHUNK]
                    gcc = d["gc"][sl]
                    decay = jnp.exp(jnp.where(incl, gcc - d["gc_row"][:, sl], -jnp.inf))
                    ln = a[:CHUNK] * decay
                    ln_bd[h, c] = jnp.where(diag_blocks, ln, 0.0)
                    ln_off[h, c] = jnp.where(off_block, ln, 0.0)
                    qk[h, c] = _bf(a[CHUNK:] * decay)
                    g_last = gcc[CHUNK - 1:CHUNK, :]
                    k_dec[h, c] = _bf(d["k"][sl] * jnp.exp(g_last - gcc))

        units = [(pr, c) for c in range(nc) for pr in pairs]

        def pair_dot(xs, ws):
            y = _dot(_cat1(*xs), _bd(*ws))
            n = ws[0].shape[1]
            return y[:, :n], y[:, n:]

        p = dict(ln_bd)
        m = {}
        for (pr, c) in units:
            bs = [_bf(ln_bd[h, c]) for h in pr]
            for h, y in zip(pr, pair_dot(bs, bs)):
                m[h, c] = y
        for _ in range(int(math.log2(half)) - 2):
            for (pr, c) in units:
                mbs = [_bf(m[h, c]) for h in pr]
                xs = [jnp.concatenate([mb, _bf(p[h, c])], axis=0) for mb, h in zip(mbs, pr)]
                for h, x2 in zip(pr, pair_dot(xs, mbs)):
                    p[h, c] = p[h, c] + m[h, c] + x2[CHUNK:]
                    m[h, c] = x2[:CHUNK]
        for (pr, c) in units:
            ys = pair_dot([_bf(p[h, c]) for h in pr], [_bf(m[h, c]) for h in pr])
            for h, y in zip(pr, ys):
                p[h, c] = p[h, c] + m[h, c] + y
        for (pr, c) in units:
            pbs = [_bf(p[h, c]) for h in pr]
            ys = pair_dot([_bf(ln_off[h, c]) for h in pr], pbs)
            ys = [ln_off[h, c] + y for h, y in zip(pr, ys)]
            zs = pair_dot(pbs, [_bf(y) for y in ys])
            for h, y, z in zip(pr, ys, zs):
                p[h, c] = p[h, c] + y + z
        sol = {}
        for c in range(nc):
            for h in heads:
                rhs = hd[h]["rhs"][c * CHUNK:(c + 1) * CHUNK]
                sol[h, c] = rhs.astype(F32) + _dot(_bf(p[h, c]), rhs)

        states = {h: state_ref[h] for h in heads}
        outs = {h: [] for h in heads}
        for c in range(nc):
            sl = slice(c * CHUNK, (c + 1) * CHUNK)
            for pr in pairs:
                xs = [jnp.concatenate([_bf(sol[h, c][:, HEAD_DIM:]), hd[h]["qeg"][sl]], axis=0) for h in pr]
                wss = pair_dot(xs, [_bf(states[h]) for h in pr])
                v_new = [_bf(sol[h, c][:, :HEAD_DIM] + ws[:CHUNK]) for h, ws in zip(pr, wss)]
                intra = pair_dot([qk[h, c] for h in pr], v_new)
                for h, ws, vn, it in zip(pr, wss, v_new, intra):
                    outs[h].append(ws[CHUNK:] + it)
                    g_last = hd[h]["gc"][(c + 1) * CHUNK - 1:(c + 1) * CHUNK, :]
                    states[h] = states[h] * jnp.exp(g_last) + _dot_tn(k_dec[h, c], vn)
        for h in heads:
            state_ref[h] = states[h]
            o = jnp.concatenate(outs[h], axis=0) if nc > 1 else outs[h][0]
            cols = slice(h * HEAD_DIM, (h + 1) * HEAD_DIM)
            o_ref[:, cols] = _head_norm_gate(o, gain_ref[...], gate_ref[:, cols])


def _deltanet(qkv, rest, alog_row, dtb_row, gain, *, layer, batch, seq, blk=512, group=4):
    blk = min(blk, seq)
    nblk = seq // blk
    t = batch * seq
    row = lambda b, s: (b * nblk + s)
    return pl.pallas_call(
        functools.partial(_dn_kernel, blk=blk, group=group),
        grid=(batch, nblk),
        in_specs=[
            pl.BlockSpec((blk, QKV_WIDTH), lambda b, s: (row(b, s), 0)),
            pl.BlockSpec((blk, DN_WIDTH), lambda b, s: (row(b, s), REST_GATE_BLOCK)),
            pl.BlockSpec((blk, LANES), lambda b, s: (row(b, s), REST_SMALL_BLOCK)),
            pl.BlockSpec((None, 1, LANES), lambda b, s: (layer, 0, 0)),
            pl.BlockSpec((None, 1, LANES), lambda b, s: (layer, 0, 0)),
            pl.BlockSpec((None, 1, HEAD_DIM), lambda b, s: (layer, 0, 0)),
        ],
        out_specs=pl.BlockSpec((blk, DN_WIDTH), lambda b, s: (row(b, s), 0)),
        out_shape=jax.ShapeDtypeStruct((t, DN_WIDTH), BF16),
        scratch_shapes=[pltpu.VMEM((DN_HEADS, HEAD_DIM, HEAD_DIM), F32)],
        compiler_params=_params("parallel", "arbitrary"),
        name="deltanet",
    )(qkv, rest, rest, alog_row, dtb_row, gain)


def _hg_level_matrix():
    r = lax.broadcasted_iota(jnp.int32, (CHUNK, CHUNK), 0)
    c = lax.broadcasted_iota(jnp.int32, (CHUNK, CHUNK), 1)
    mats = [c <= r]
    for m in HG_LEVELS:
        mats.append(c <= (r // (2 * m)) * (2 * m) + m - 1)
    mat = jnp.concatenate([jnp.where(x, 1.0, 0.0).astype(BF16) for x in mats], axis=0)
    return _cat1(mat, mat)


def _hg_kernel(x_ref, lbt_ref, gain_ref, o_ref, state_ref, *, blk, layer):
    s = pl.program_id(1)

    @pl.when(s == 0)
    def _():
        state_ref[...] = jnp.zeros_like(state_ref)

    tbl = lbt_ref[...]
    e = jnp.exp(tbl - jnp.max(tbl, axis=0, keepdims=True))
    sm = e / jnp.sum(e, axis=0, keepdims=True)
    lb = jnp.zeros((1, HG_WIDTH), F32)
    for l in range(1, layer + 1):
        lb = lb + sm[l:l + 1, :]

    level_mat = _hg_level_matrix()
    row = lax.broadcasted_iota(jnp.int32, (CHUNK, CHUNK), 0)
    col = lax.broadcasted_iota(jnp.int32, (CHUNK, CHUNK), 1)
    rowv = lax.broadcasted_iota(jnp.int32, (CHUNK, HEAD_DIM), 0)
    eye = row == col
    pair_mask = [((row // (2 * m)) == (col // (2 * m))) & ((row // m) % 2 == 1) & ((col // m) % 2 == 0)
                 for m in HG_LEVELS]
    lower_rows = [(rowv // m) % 2 == 1 for m in HG_LEVELS]
    pairs = [(h, h + 1) for h in range(0, HG_HEADS, 2)]

    for c in range(blk // CHUNK):
        sl = slice(c * CHUNK, (c + 1) * CHUNK)
        z = x_ref[sl, HG_WIDTH:2 * HG_WIDTH]
        sig = _sigmoid(z)
        log2_f = jnp.log(lb + (1.0 - lb) * sig) * LOG2_E
        hk_all = (1.0 - lb) * (1.0 - sig)
        lf_hi, lf_lo = _split_bf16(log2_f)
        cums = _dot(level_mat, jnp.concatenate([lf_hi, lf_lo], axis=0))
        for pr in pairs:
            q, k, v, b, state_t = {}, {}, {}, {}, {}
            for h in pr:
                cols = slice(h * HEAD_DIM, (h + 1) * HEAD_DIM)
                q[h] = _silu(x_ref[sl, cols]) * (HEAD_DIM ** -0.5)
                k[h] = hk_all[:, cols]
                v[h] = _bf(x_ref[sl, 2 * HG_WIDTH + h * HEAD_DIM:2 * HG_WIDTH + (h + 1) * HEAD_DIM])
                b[h] = cums[0:CHUNK, cols]
                state_t[h] = state_ref[h]
            scores = [_dot_nt(_cat1(*[_bf(q[h]) for h in pr]), _bd(*[_bf(k[h]) for h in pr]))]
            for li in range(len(HG_LEVELS)):
                xs = []
                for h in pr:
                    ref = cums[(li + 1) * CHUNK:(li + 2) * CHUNK, h * HEAD_DIM:(h + 1) * HEAD_DIM]
                    scale = jnp.exp2(-jnp.abs(b[h] - ref))
                    xs.append(_bf(jnp.where(lower_rows[li], q[h], k[h]) * scale))
                scores.append(_dot_nt(_cat1(*xs), _bd(*xs)))
            att = []
            for i, h in enumerate(pr):
                a = jnp.zeros((CHUNK, CHUNK), F32)
                for li in range(len(HG_LEVELS)):
                    a = jnp.where(pair_mask[li], scores[li + 1][:, i * CHUNK:(i + 1) * CHUNK], a)
                att.append(_bf(jnp.where(eye, scores[0][:, i * CHUNK:(i + 1) * CHUNK], a)))
            intra = _dot(_cat1(*att), _bd(*[v[h] for h in pr]))
            inter = _dot_nt(_cat1(*[_bf(q[h] * jnp.exp2(b[h])) for h in pr]),
                            _bd(*[_bf(state_t[h]) for h in pr]))
            o2 = intra + inter
            for i, h in enumerate(pr):
                b_last = b[h][CHUNK - 1:CHUNK, :]
                k_dec = k[h] * jnp.exp2(b_last - b[h])
                state_ref[h] = state_t[h] * jnp.exp2(b_last) + _dot_tn(v[h], _bf(k_dec))
                gate = x_ref[sl, 3 * HG_WIDTH + h * HEAD_DIM:3 * HG_WIDTH + (h + 1) * HEAD_DIM]
                o_ref[sl, h * HEAD_DIM:(h + 1) * HEAD_DIM] = _head_norm_gate(
                    o2[:, i * HEAD_DIM:(i + 1) * HEAD_DIM], gain_ref[...], gate)


def _hgrn2(rest, lb_table, gain, *, layer, batch, seq, blk=512):
    blk = min(blk, seq)
    nblk = seq // blk
    t = batch * seq
    return pl.pallas_call(
        functools.partial(_hg_kernel, blk=blk, layer=layer),
        grid=(batch, nblk),
        in_specs=[
            pl.BlockSpec((blk, 4 * HG_WIDTH), lambda b, s: (b * nblk + s, 0)),
            pl.BlockSpec((DEPTH, HG_WIDTH), lambda b, s: (0, 0)),
            pl.BlockSpec((None, 1, HEAD_DIM), lambda b, s: (layer, 0, 0)),
        ],
        out_specs=pl.BlockSpec((blk, HG_WIDTH), lambda b, s: (b * nblk + s, 0)),
        out_shape=jax.ShapeDtypeStruct((t, HG_WIDTH), BF16),
        scratch_shapes=[pltpu.VMEM((HG_HEADS, HEAD_DIM, HEAD_DIM), F32)],
        compiler_params=_params("parallel", "arbitrary"),
        name="hgrn2",
    )(rest, lb_table, gain)


def _rope_kernel(cos_ref, sin_ref):
    shape = cos_ref.shape
    half = HEAD_DIM // 2
    pos = lax.broadcasted_iota(jnp.int32, shape, 0).astype(F32)
    lane = lax.broadcasted_iota(jnp.int32, shape, 1)
    inv_freq = jnp.exp((lane % half).astype(F32) * (-math.log(ROPE_THETA) / half))
    ang = pos * inv_freq
    cos_ref[...] = jnp.cos(ang)
    sin = jnp.sin(ang)
    sin_ref[...] = jnp.where(lane < half, -sin, sin)


def _rope_tables(seq):
    shape = jax.ShapeDtypeStruct((seq, HEAD_DIM), F32)
    return pl.pallas_call(_rope_kernel, out_shape=(shape, shape), name="rope_tables")()


def _rt_kernel(x_ref, cos_ref, sin_ref, gain_ref, o_ref, state_ref, *, blk):
    s = pl.program_id(1)

    @pl.when(s == 0)
    def _():
        state_ref[...] = jnp.zeros_like(state_ref)

    cos = cos_ref[...]
    sin = sin_ref[...]
    row = lax.broadcasted_iota(jnp.int32, (blk, blk), 0)
    col = lax.broadcasted_iota(jnp.int32, (blk, blk), 1)
    dist = (row - col).astype(F32)
    pos = lax.broadcasted_iota(jnp.int32, (blk, HEAD_DIM), 0).astype(F32)

    def rotary(x):
        return x * cos + pltpu.roll(x, HEAD_DIM // 2, axis=1) * sin

    for h in range(RT_HEADS):
        log_gamma = math.log(1.0 - 2.0 ** (-5.0 - h))
        cols = slice(h * HEAD_DIM, (h + 1) * HEAD_DIM)
        q = rotary(x_ref[:, cols]) * (HEAD_DIM ** -0.5)
        k = rotary(x_ref[:, RT_WIDTH + h * HEAD_DIM:RT_WIDTH + (h + 1) * HEAD_DIM])
        v = x_ref[:, 2 * RT_WIDTH + h * HEAD_DIM:2 * RT_WIDTH + (h + 1) * HEAD_DIM]
        vb = _bf(v)
        dmat = jnp.exp(jnp.where(col <= row, dist * log_gamma, -jnp.inf))
        att = _dot_nt(_bf(q), _bf(k)) * dmat
        state = state_ref[h]
        o = _dot(_bf(att), vb) + _dot(_bf(q * jnp.exp((pos + 1.0) * log_gamma)), _bf(state))
        k_dec = k * jnp.exp((blk - 1.0 - pos) * log_gamma)
        state_ref[h] = state * math.exp(blk * log_gamma) + _dot_tn(_bf(k_dec), vb)
        gate = x_ref[:, 3 * RT_WIDTH + h * HEAD_DIM:3 * RT_WIDTH + (h + 1) * HEAD_DIM]
        o_ref[:, cols] = _head_norm_gate(o, gain_ref[...], gate)


def _retention(rest, cos_tab, sin_tab, gain, *, layer, batch, seq, blk=256):
    blk = min(blk, seq)
    nblk = seq // blk
    t = batch * seq
    return pl.pallas_call(
        functools.partial(_rt_kernel, blk=blk),
        grid=(batch, nblk),
        in_specs=[
            pl.BlockSpec((blk, 4 * RT_WIDTH), lambda b, s: (b * nblk + s, 1)),
            pl.BlockSpec((blk, HEAD_DIM), lambda b, s: (s, 0)),
            pl.BlockSpec((blk, HEAD_DIM), lambda b, s: (s, 0)),
            pl.BlockSpec((None, 1, HEAD_DIM), lambda b, s: (layer, 0, 0)),
        ],
        out_specs=pl.BlockSpec((blk, RT_WIDTH), lambda b, s: (b * nblk + s, 0)),
        out_shape=jax.ShapeDtypeStruct((t, RT_WIDTH), BF16),
        scratch_shapes=[pltpu.VMEM((RT_HEADS, HEAD_DIM, HEAD_DIM), F32)],
        compiler_params=_params("parallel", "arbitrary"),
        name="retention",
    )(rest, cos_tab, sin_tab, gain)


def _outproj_kernel(h_ref, dn_ref, hg_ref, rt_ref, w_ref, o_ref):
    acc = _dot(dn_ref[...], w_ref[0:DN_WIDTH, :])
    acc += _dot(hg_ref[...], w_ref[DN_WIDTH:DN_WIDTH + HG_WIDTH, :])
    acc += _dot(rt_ref[...], w_ref[DN_WIDTH + HG_WIDTH:, :])
    o_ref[...] = h_ref[...] + acc


def _outproj(h, o_dn, o_hg, o_rt, w, *, layer, tm=512):
    t, d = h.shape
    tm = min(tm, t)
    return pl.pallas_call(
        _outproj_kernel,
        grid=(t // tm,),
        in_specs=[
            pl.BlockSpec((tm, d), lambda i: (i, 0)),
            pl.BlockSpec((tm, DN_WIDTH), lambda i: (i, 0)),
            pl.BlockSpec((tm, HG_WIDTH), lambda i: (i, 0)),
            pl.BlockSpec((tm, RT_WIDTH), lambda i: (i, 0)),
            pl.BlockSpec((None, d, d), lambda i: (layer, 0, 0)),
        ],
        out_specs=pl.BlockSpec((tm, d), lambda i: (i, 0)),
        out_shape=jax.ShapeDtypeStruct((t, d), F32),
        compiler_params=_params("parallel"),
        name="outproj",
    )(h, o_dn, o_hg, o_rt, w)


def _ple_kernel(h_ref, nw_ref, wg_ref, p_ref, wp_ref, fn_ref, o_ref, *, final):
    x = h_ref[...]
    gate = _sigmoid(_dot(_bf(_rms_norm(x, nw_ref[...])), wg_ref[...]))
    y = x + gate * _dot(_bf(p_ref[...]), wp_ref[...])
    if final:
        y = _rms_norm(y, fn_ref[...])
    o_ref[...] = y


def _ple(h, nw, wg, p, wp, fn, *, layer, final, tm=512):
    t, d = h.shape
    tm = min(tm, t)
    return pl.pallas_call(
        functools.partial(_ple_kernel, final=final),
        grid=(t // tm,),
        in_specs=[
            pl.BlockSpec((tm, d), lambda i: (i, 0)),
            pl.BlockSpec((None, 1, d), lambda i: (layer, 0, 0)),
            pl.BlockSpec((None, d, d), lambda i: (layer, 0, 0)),
            pl.BlockSpec((None, tm, PLE_DIM), lambda i: (layer, i, 0)),
            pl.BlockSpec((None, PLE_DIM, d), lambda i: (layer, 0, 0)),
            pl.BlockSpec((1, d), lambda i: (0, 0)),
        ],
        out_specs=pl.BlockSpec((tm, d), lambda i: (i, 0)),
        out_shape=jax.ShapeDtypeStruct((t, d), F32),
        compiler_params=_params("parallel"),
        name="ple",
    )(h, nw, wg, p, wp, fn)


def _rest_columns(w):
    c0 = QKV_WIDTH
    c1 = c0 + DN_WIDTH
    c2 = c1 + 2 * DN_HEADS
    pad = jnp.zeros(w.shape[:2] + (REST_WIDTH - REST_MAIN - 2 * DN_HEADS,), w.dtype)
    return jnp.concatenate([w[..., c2:], w[..., c0:c1], w[..., c1:c2], pad], axis=-1)


def _lane_rows(x, offset):
    n = x.shape[1]
    return jnp.pad(x.astype(F32), ((0, 0), (offset, LANES - offset - n)))[:, None, :]


def kernel(x, p, ffn1_norm, ffn1_w_gate_up, ffn1_w_down, mix_norm, w_in, dn_conv, dn_a_log, dn_dt_bias, dn_out_norm, hg_lower_bounds, hg_out_norm, rt_out_norm, w_out, ffn2_norm, ffn2_w_gate_up, ffn2_w_down, ple_norm, ple_w_gate, ple_w_proj, final_norm):
    batch, seq, d = x.shape
    t = batch * seq
    h = x.reshape(t, d)
    cos_tab, sin_tab = _rope_tables(seq)
    rows = lambda v: v.astype(F32)[:, None, :]
    ffn1_wgu, ffn1_wd = _bf(ffn1_w_gate_up), _bf(ffn1_w_down)
    ffn2_wgu, ffn2_wd = _bf(ffn2_w_gate_up), _bf(ffn2_w_down)
    w_qkv = _bf(w_in[..., :QKV_WIDTH])
    w_rest = _bf(_rest_columns(w_in))
    w_out_bf, ple_wg, ple_wp = _bf(w_out), _bf(ple_w_gate), _bf(ple_w_proj)
    ffn1_nw, ffn2_nw, mix_nw, ple_nw = rows(ffn1_norm), rows(ffn2_norm), rows(mix_norm), rows(ple_norm)
    dn_gain, hg_gain, rt_gain = rows(dn_out_norm), rows(hg_out_norm), rows(rt_out_norm)
    alog_rows, dtb_rows = _lane_rows(dn_a_log, DN_HEADS), _lane_rows(dn_dt_bias, DN_HEADS)
    p_tok = p.reshape(DEPTH, t, PLE_DIM)
    final_nw = final_norm.astype(F32)[None, :]
    for i in range(DEPTH):
        h = _ffn(h, ffn1_nw, ffn1_wgu, ffn1_wd, layer=i)
        qkv = _qkvproj(h, mix_nw, w_qkv, dn_conv, layer=i, seq=seq)
        rest = _restproj(h, mix_nw, w_rest, layer=i)
        o_dn = _deltanet(qkv, rest, alog_rows, dtb_rows, dn_gain, layer=i, batch=batch, seq=seq)
        o_hg = _hgrn2(rest, hg_lower_bounds, hg_gain, layer=i, batch=batch, seq=seq)
        o_rt = _retention(rest, cos_tab, sin_tab, rt_gain, layer=i, batch=batch, seq=seq)
        h = _outproj(h, o_dn, o_hg, o_rt, w_out_bf, layer=i)
        h = _ffn(h, ffn2_nw, ffn2_wgu, ffn2_wd, layer=i)
        h = _ple(h, ple_nw, ple_wg, p_tok, ple_wp, final_nw, layer=i, final=(i == DEPTH - 1))
    return h.reshape(batch, seq, d)
```

```python
import functools
import math

import jax
import jax.numpy as jnp
from jax import lax
from jax.experimental import pallas as pl
from jax.experimental.pallas import tpu as pltpu

F32 = jnp.float32
BF16 = jnp.bfloat16

D_MODEL = 2048
D_FF = 5632
DEPTH = 2
DN_HEADS = 8
HG_HEADS = 4
RT_HEADS = 4
HEAD_DIM = 128
DN_WIDTH = DN_HEADS * HEAD_DIM
HG_WIDTH = HG_HEADS * HEAD_DIM
RT_WIDTH = RT_HEADS * HEAD_DIM
QKV_WIDTH = 3 * DN_WIDTH
CONV_WIDTH = 4
PLE_DIM = 256
ROPE_THETA = 10000.0
NORM_EPS = 1e-6
LOG2_E = math.log2(math.e)

LANES = 128
SUBLANES = 8
VMEM_LIMIT = 56 * 1024 * 1024

REST_MAIN = 4 * HG_WIDTH + 4 * RT_WIDTH + DN_WIDTH
REST_WIDTH = REST_MAIN + 256
REST_GATE_BLOCK = (4 * HG_WIDTH + 4 * RT_WIDTH) // DN_WIDTH
REST_SMALL_BLOCK = REST_MAIN // LANES

CHUNK = 128
HG_LEVELS = (64, 32, 16, 8, 4, 2, 1)


def _dot(a, b):
    return jnp.dot(a, b, preferred_element_type=F32)


def _dot_nt(a, b):
    return lax.dot_general(a, b, (((1,), (1,)), ((), ())), preferred_element_type=F32)


def _dot_tn(a, b):
    return lax.dot_general(a, b, (((0,), (0,)), ((), ())), preferred_element_type=F32)


def _bf(x):
    return x.astype(BF16)


def _sigmoid(x):
    return 1.0 / (1.0 + jnp.exp(-x))


def _silu(x):
    return x * _sigmoid(x)


def _rms_norm(x, w):
    return x * lax.rsqrt(jnp.mean(x * x, axis=-1, keepdims=True) + NORM_EPS) * w


def _split_bf16(x):
    hi = x.astype(BF16)
    lo = (x - hi.astype(F32)).astype(BF16)
    return hi, lo


def _params(*sem):
    return pltpu.CompilerParams(dimension_semantics=sem, vmem_limit_bytes=VMEM_LIMIT)


def _ffn_kernel(h_ref, nw_ref, wg_ref, wu_ref, wd_ref, o_ref, hn_ref, *, nf):
    j = pl.program_id(1)

    def partial_sum():
        hn = hn_ref[...]
        g = _dot(hn, wg_ref[...])
        u = _dot(hn, wu_ref[...])
        return _dot(_bf(_silu(g) * u), wd_ref[...])

    @pl.when(j == 0)
    def _():
        hn_ref[...] = _bf(_rms_norm(h_ref[...], nw_ref[...]))
        o_ref[...] = partial_sum()

    @pl.when((j > 0) & (j < nf - 1))
    def _():
        o_ref[...] += partial_sum()

    @pl.when(j == nf - 1)
    def _():
        o_ref[...] = h_ref[...] + 0.5 * (o_ref[...] + partial_sum())


def _ffn(h, nw, wgu, wd, *, layer, tm=1024, tf=512):
    t, d = h.shape
    tm = min(tm, t)
    nf = D_FF // tf
    return pl.pallas_call(
        functools.partial(_ffn_kernel, nf=nf),
        grid=(t // tm, nf),
        in_specs=[
            pl.BlockSpec((tm, d), lambda i, j: (i, 0)),
            pl.BlockSpec((None, 1, d), lambda i, j: (layer, 0, 0)),
            pl.BlockSpec((None, d, tf), lambda i, j: (layer, 0, j)),
            pl.BlockSpec((None, d, tf), lambda i, j: (layer, 0, j + nf)),
            pl.BlockSpec((None, tf, d), lambda i, j: (layer, j, 0)),
        ],
        out_specs=pl.BlockSpec((tm, d), lambda i, j: (i, 0)),
        out_shape=jax.ShapeDtypeStruct((t, d), F32),
        scratch_shapes=[pltpu.VMEM((tm, d), BF16)],
        compiler_params=_params("parallel", "arbitrary"),
        name="ffn",
    )(h, nw, wgu, wgu, wd)


def _qkv_kernel(h_ref, nw_ref, w_ref, cw_ref, o_ref, hn_ref, tail_ref, *, tiles_per_seq, cb):
    i = pl.program_id(0)
    first = (i % tiles_per_seq) == 0
    tm = h_ref.shape[0]
    hn_ref[...] = _bf(_rms_norm(h_ref[...], nw_ref[...]))
    nchunk = QKV_WIDTH // cb
    y_next = _dot(hn_ref[...], w_ref[:, 0:cb])
    for c in range(nchunk):
        cs = slice(c * cb, (c + 1) * cb)
        y = y_next
        if c + 1 < nchunk:
            y_next = _dot(hn_ref[...], w_ref[:, (c + 1) * cb:(c + 2) * cb])
        tail = jnp.where(first, 0.0, tail_ref[:, cs])
        tail_ref[:, cs] = y[tm - SUBLANES:, :]
        z = jnp.concatenate([tail, y], axis=0)
        acc = y * cw_ref[CONV_WIDTH - 1:CONV_WIDTH, cs]
        for j in range(CONV_WIDTH - 1):
            shifted = pltpu.roll(z, CONV_WIDTH - 1 - j, axis=0)[SUBLANES:, :]
            acc = acc + shifted * cw_ref[j:j + 1, cs]
        x = _silu(acc)
        for hh in range(cb // HEAD_DIM):
            col = c * cb + hh * HEAD_DIM
            xh = x[:, hh * HEAD_DIM:(hh + 1) * HEAD_DIM]
            if col < 2 * DN_WIDTH:
                r = lax.rsqrt(jnp.sum(xh * xh, axis=-1, keepdims=True) + NORM_EPS)
                if col < DN_WIDTH:
                    r = r * (HEAD_DIM ** -0.5)
                xh = xh * r
            o_ref[:, col:col + HEAD_DIM] = xh
        zero = jnp.minimum(jnp.abs(xh[0:2 * SUBLANES, :]), 0.0)
        hn_ref[0:2 * SUBLANES, 0:HEAD_DIM] += _bf(zero)


def _qkvproj(h, nw, w, conv_w, *, layer, seq, tm=512, cb=512):
    t, d = h.shape
    tm = min(tm, seq)
    return pl.pallas_call(
        functools.partial(_qkv_kernel, tiles_per_seq=seq // tm, cb=cb),
        grid=(t // tm,),
        in_specs=[
            pl.BlockSpec((tm, d), lambda i: (i, 0)),
            pl.BlockSpec((None, 1, d), lambda i: (layer, 0, 0)),
            pl.BlockSpec((None, d, QKV_WIDTH), lambda i: (layer, 0, 0)),
            pl.BlockSpec((None, CONV_WIDTH, QKV_WIDTH), lambda i: (layer, 0, 0)),
        ],
        out_specs=pl.BlockSpec((tm, QKV_WIDTH), lambda i: (i, 0)),
        out_shape=jax.ShapeDtypeStruct((t, QKV_WIDTH), F32),
        scratch_shapes=[pltpu.VMEM((tm, d), BF16), pltpu.VMEM((SUBLANES, QKV_WIDTH), F32)],
        compiler_params=_params("arbitrary"),
        name="qkvproj",
    )(h, nw, w, conv_w)


def _restproj_kernel(h_ref, nw_ref, w_ref, o_ref, hn_ref):
    @pl.when(pl.program_id(1) == 0)
    def _():
        hn_ref[...] = _bf(_rms_norm(h_ref[...], nw_ref[...]))

    o_ref[...] = _dot(hn_ref[...], w_ref[...])


def _restproj(h, nw, w, *, layer, tm=1024, tn=1792):
    t, d = h.shape
    tm = min(tm, t)
    n = w.shape[-1]
    return pl.pallas_call(
        _restproj_kernel,
        grid=(t // tm, n // tn),
        in_specs=[
            pl.BlockSpec((tm, d), lambda i, j: (i, 0)),
            pl.BlockSpec((None, 1, d), lambda i, j: (layer, 0, 0)),
            pl.BlockSpec((None, d, tn), lambda i, j: (layer, 0, j)),
        ],
        out_specs=pl.BlockSpec((tm, tn), lambda i, j: (i, j)),
        out_shape=jax.ShapeDtypeStruct((t, n), F32),
        scratch_shapes=[pltpu.VMEM((tm, d), BF16)],
        compiler_params=_params("parallel", "arbitrary"),
        name="restproj",
    )(h, nw, w)


def _chunk_tril_bf16(blk):
    r = lax.broadcasted_iota(jnp.int32, (blk, blk), 0)
    c = lax.broadcasted_iota(jnp.int32, (blk, blk), 1)
    return jnp.where((r // CHUNK == c // CHUNK) & (c <= r), 1.0, 0.0).astype(BF16)


def _cat1(*xs):
    return jnp.concatenate(xs, axis=1)


def _bd(a, b):
    z = jnp.zeros_like(a)
    return jnp.concatenate([_cat1(a, z), _cat1(z, b)], axis=0)


def _head_norm_gate(o, gain, gate):
    o = o * lax.rsqrt(jnp.mean(o * o, axis=-1, keepdims=True) + NORM_EPS) * gain
    return _bf(o * _silu(gate))


def _dn_body(qkv_ref, gate_ref, small_ref, alog_ref, dtb_ref, gain_ref, o_ref, state_ref, *, blk, group, col0):
    nc = blk // CHUNK
    half = CHUNK // 2

    small = small_ref[...]
    beta_all = _sigmoid(small)
    x = small + dtb_ref[...]
    softplus = jnp.maximum(x, 0.0) + jnp.log(1.0 + jnp.exp(-jnp.abs(x)))
    g_all = -(jnp.exp(alog_ref[...]) * softplus)
    tril = _chunk_tril_bf16(blk)
    g_hi, g_lo = _split_bf16(g_all)
    gc_all = _dot(tril, g_hi) + _dot(tril, g_lo)
    gc_t = gc_all.T

    row = lax.broadcasted_iota(jnp.int32, (CHUNK, CHUNK), 0)
    col = lax.broadcasted_iota(jnp.int32, (CHUNK, CHUNK), 1)
    incl = col <= row
    same_half = (row // half) == (col // half)
    diag_blocks = same_half & (col < row)
    off_block = (row >= half) & (col < half)

    for g0 in range(0, DN_HEADS, group):
        heads = list(range(g0, g0 + group))
        pairs = [(heads[i], heads[i + 1]) for i in range(0, group, 2)]

        hd = {}
        for h in heads:
            q = qkv_ref[:, h * HEAD_DIM:(h + 1) * HEAD_DIM]
            k = qkv_ref[:, DN_WIDTH + h * HEAD_DIM:DN_WIDTH + (h + 1) * HEAD_DIM]
            v = qkv_ref[:, 2 * DN_WIDTH + h * HEAD_DIM:2 * DN_WIDTH + (h + 1) * HEAD_DIM]
            beta = jnp.broadcast_to(beta_all[:, h:h + 1], (blk, HEAD_DIM))
            gc = jnp.broadcast_to(gc_all[:, DN_HEADS + h:DN_HEADS + h + 1], (blk, HEAD_DIM))
            eg = jnp.exp(gc)
            kbn = -(k * beta)
            hd[h] = dict(k=k, gc=gc, kb=_bf(k), lhs=(_bf(kbn), _bf(q)),
                         rhs=_bf(jnp.concatenate([v * beta, kbn * eg], axis=1)),
                         qeg=_bf(q * eg), gc_row=gc_t[DN_HEADS + h:DN_HEADS + h + 1, :])
        yield

        ln_bd, ln_off, qk, k_dec = {}, {}, {}, {}
        for c in range(nc):
            sl = slice(c * CHUNK, (c + 1) * CHUNK)
            for pr in pairs:
                lhs = _cat1(*[jnp.concatenate([hd[h]["lhs"][0][sl], hd[h]["lhs"][1][sl]], axis=0) for h in pr])
                a2 = _dot_nt(lhs, _bd(hd[pr[0]]["kb"][sl], hd[pr[1]]["kb"][sl]))
                for i, h in enumerate(pr):
                    d = hd[h]
                    a = a2[:, i * CHUNK:(i + 1) * CHUNK]
                    gcc = d["gc"][sl]
                    decay = jnp.exp(jnp.where(incl, gcc - d["gc_row"][:, sl], -jnp.inf))
                    ln = a[:CHUNK] * decay
                    ln_bd[h, c] = jnp.where(diag_blocks, ln, 0.0)
                    ln_off[h, c] = jnp.where(off_block, ln, 0.0)
                    qk[h, c] = _bf(a[CHUNK:] * decay)
                    g_last = gcc[CHUNK - 1:CHUNK, :]
                    k_dec[h, c] = _bf(d["k"][sl] * jnp.exp(g_last - gcc))
            yield

        units = [(pr, c) for c in range(nc) for pr in pairs]

        def pair_dot(xs, ws):
            y = _dot(_cat1(*xs), _bd(*ws))
            n = ws[0].shape[1]
            return y[:, :n], y[:, n:]

        p = dict(ln_bd)
        m = {}
        for (pr, c) in units:
            bs = [_bf(ln_bd[h, c]) for h in pr]
            for h, y in zip(pr, pair_dot(bs, bs)):
                m[h, c] = y
        yield
        for _ in range(int(math.log2(half)) - 2):
            for (pr, c) in units:
                mbs = [_bf(m[h, c]) for h in pr]
                xs = [jnp.concatenate([mb, _bf(p[h, c])], axis=0) for mb, h in zip(mbs, pr)]
                for h, x2 in zip(pr, pair_dot(xs, mbs)):
                    p[h, c] = p[h, c] + m[h, c] + x2[CHUNK:]
                    m[h, c] = x2[:CHUNK]
            yield
        for (pr, c) in units:
            ys = pair_dot([_bf(p[h, c]) for h in pr], [_bf(m[h, c]) for h in pr])
            for h, y in zip(pr, ys):
                p[h, c] = p[h, c] + m[h, c] + y
        yield
        for (pr, c) in units:
            pbs = [_bf(p[h, c]) for h in pr]
            ys = pair_dot([_bf(ln_off[h, c]) for h in pr], pbs)
            ys = [ln_off[h, c] + y for h, y in zip(pr, ys)]
            zs = pair_dot(pbs, [_bf(y) for y in ys])
            for h, y, z in zip(pr, ys, zs):
                p[h, c] = p[h, c] + y + z
        yield
        sol = {}
        for c in range(nc):
            for h in heads:
                rhs = hd[h]["rhs"][c * CHUNK:(c + 1) * CHUNK]
                sol[h, c] = rhs.astype(F32) + _dot(_bf(p[h, c]), rhs)
        yield

        states = {h: state_ref[h] for h in heads}
        outs = {h: [] for h in heads}
        for c in range(nc):
            sl = slice(c * CHUNK, (c + 1) * CHUNK)
            for pr in pairs:
                xs = [jnp.concatenate([_bf(sol[h, c][:, HEAD_DIM:]), hd[h]["qeg"][sl]], axis=0) for h in pr]
                wss = pair_dot(xs, [_bf(states[h]) for h in pr])
                v_new = [_bf(sol[h, c][:, :HEAD_DIM] + ws[:CHUNK]) for h, ws in zip(pr, wss)]
                intra = pair_dot([qk[h, c] for h in pr], v_new)
                for h, ws, vn, it in zip(pr, wss, v_new, intra):
                    outs[h].append(ws[CHUNK:] + it)
                    g_last = hd[h]["gc"][(c + 1) * CHUNK - 1:(c + 1) * CHUNK, :]
                    states[h] = states[h] * jnp.exp(g_last) + _dot_tn(k_dec[h, c], vn)
            yield
        for h in heads:
            state_ref[h] = states[h]
            o = jnp.concatenate(outs[h], axis=0) if nc > 1 else outs[h][0]
            cols = slice(h * HEAD_DIM, (h + 1) * HEAD_DIM)
            o_ref[:, col0 + h * HEAD_DIM:col0 + (h + 1) * HEAD_DIM] = _head_norm_gate(
                o, gain_ref[...], gate_ref[:, cols])


def _hg_level_matrix():
    r = lax.broadcasted_iota(jnp.int32, (CHUNK, CHUNK), 0)
    c = lax.broadcasted_iota(jnp.int32, (CHUNK, CHUNK), 1)
    mats = [c <= r]
    for m in HG_LEVELS:
        mats.append(c <= (r // (2 * m)) * (2 * m) + m - 1)
    mat = jnp.concatenate([jnp.where(x, 1.0, 0.0).astype(BF16) for x in mats], axis=0)
    return _cat1(mat, mat)


def _hg_body(x_ref, lbt_ref, gain_ref, o_ref, state_ref, *, blk, layer, col0):
    tbl = lbt_ref[...]
    e = jnp.exp(tbl - jnp.max(tbl, axis=0, keepdims=True))
    sm = e / jnp.sum(e, axis=0, keepdims=True)
    lb = jnp.zeros((1, HG_WIDTH), F32)
    for l in range(1, layer + 1):
        lb = lb + sm[l:l + 1, :]

    level_mat = _hg_level_matrix()
    row = lax.broadcasted_iota(jnp.int32, (CHUNK, CHUNK), 0)
    col = lax.broadcasted_iota(jnp.int32, (CHUNK, CHUNK), 1)
    rowv = lax.broadcasted_iota(jnp.int32, (CHUNK, HEAD_DIM), 0)
    eye = row == col
    pair_mask = [((row // (2 * m)) == (col // (2 * m))) & ((row // m) % 2 == 1) & ((col // m) % 2 == 0)
                 for m in HG_LEVELS]
    lower_rows = [(rowv // m) % 2 == 1 for m in HG_LEVELS]
    pairs = [(h, h + 1) for h in range(0, HG_HEADS, 2)]

    for c in range(blk // CHUNK):
        sl = slice(c * CHUNK, (c + 1) * CHUNK)
        z = x_ref[sl, HG_WIDTH:2 * HG_WIDTH]
        sig = _sigmoid(z)
        log2_f = jnp.log(lb + (1.0 - lb) * sig) * LOG2_E
        hk_all = (1.0 - lb) * (1.0 - sig)
        lf_hi, lf_lo = _split_bf16(log2_f)
        cums = _dot(level_mat, jnp.concatenate([lf_hi, lf_lo], axis=0))
        yield
        for pr in pairs:
            q, k, v, b, state_t = {}, {}, {}, {}, {}
            for h in pr:
                cols = slice(h * HEAD_DIM, (h + 1) * HEAD_DIM)
                q[h] = _silu(x_ref[sl, cols]) * (HEAD_DIM ** -0.5)
                k[h] = hk_all[:, cols]
                v[h] = _bf(x_ref[sl, 2 * HG_WIDTH + h * HEAD_DIM:2 * HG_WIDTH + (h + 1) * HEAD_DIM])
                b[h] = cums[0:CHUNK, cols]
                state_t[h] = state_ref[h]
            scores = [_dot_nt(_cat1(*[_bf(q[h]) for h in pr]), _bd(*[_bf(k[h]) for h in pr]))]
            for li in range(len(HG_LEVELS)):
                xs = []
                for h in pr:
                    ref = cums[(li + 1) * CHUNK:(li + 2) * CHUNK, h * HEAD_DIM:(h + 1) * HEAD_DIM]
                    scale = jnp.exp2(-jnp.abs(b[h] - ref))
                    xs.append(_bf(jnp.where(lower_rows[li], q[h], k[h]) * scale))
                scores.append(_dot_nt(_cat1(*xs), _bd(*xs)))
                if li % 2 == 1:
                    yield
            att = []
            for i, h in enumerate(pr):
                a = jnp.zeros((CHUNK, CHUNK), F32)
                for li in range(len(HG_LEVELS)):
                    a = jnp.where(pair_mask[li], scores[li + 1][:, i * CHUNK:(i + 1) * CHUNK], a)
                att.append(_bf(jnp.where(eye, scores[0][:, i * CHUNK:(i + 1) * CHUNK], a)))
            intra = _dot(_cat1(*att), _bd(*[v[h] for h in pr]))
            inter = _dot_nt(_cat1(*[_bf(q[h] * jnp.exp2(b[h])) for h in pr]),
                            _bd(*[_bf(state_t[h]) for h in pr]))
            o2 = intra + inter
            yield
            for i, h in enumerate(pr):
                b_last = b[h][CHUNK - 1:CHUNK, :]
                k_dec = k[h] * jnp.exp2(b_last - b[h])
                state_ref[h] = state_t[h] * jnp.exp2(b_last) + _dot_tn(v[h], _bf(k_dec))
                gate = x_ref[sl, 3 * HG_WIDTH + h * HEAD_DIM:3 * HG_WIDTH + (h + 1) * HEAD_DIM]
                o_ref[sl, col0 + h * HEAD_DIM:col0 + (h + 1) * HEAD_DIM] = _head_norm_gate(
                    o2[:, i * HEAD_DIM:(i + 1) * HEAD_DIM], gain_ref[...], gate)


def _rope_kernel(cos_ref, sin_ref):
    shape = cos_ref.shape
    half = HEAD_DIM // 2
    pos = lax.broadcasted_iota(jnp.int32, shape, 0).astype(F32)
    lane = lax.broadcasted_iota(jnp.int32, shape, 1)
    inv_freq = jnp.exp((lane % half).astype(F32) * (-math.log(ROPE_THETA) / half))
    ang = pos * inv_freq
    cos_ref[...] = jnp.cos(ang)
    sin = jnp.sin(ang)
    sin_ref[...] = jnp.where(lane < half, -sin, sin)


def _rope_tables(seq):
    shape = jax.ShapeDtypeStruct((seq, HEAD_DIM), F32)
    return pl.pallas_call(_rope_kernel, out_shape=(shape, shape), name="rope_tables")()


RT_CHUNK = 256


def _rt_body(x_ref, cos_ref, sin_ref, gain_ref, o_ref, state_ref, *, blk, col0):
    n = RT_CHUNK
    row = lax.broadcasted_iota(jnp.int32, (n, n), 0)
    col = lax.broadcasted_iota(jnp.int32, (n, n), 1)
    dist = (row - col).astype(F32)
    pos = lax.broadcasted_iota(jnp.int32, (n, HEAD_DIM), 0).astype(F32)

    for c in range(blk // n):
        sl = slice(c * n, (c + 1) * n)
        cos = cos_ref[sl, :]
        sin = sin_ref[sl, :]

        def rotary(x):
            return x * cos + pltpu.roll(x, HEAD_DIM // 2, axis=1) * sin

        for h in range(RT_HEADS):
            log_gamma = math.log(1.0 - 2.0 ** (-5.0 - h))
            cols = slice(h * HEAD_DIM, (h + 1) * HEAD_DIM)
            q = rotary(x_ref[sl, cols]) * (HEAD_DIM ** -0.5)
            k = rotary(x_ref[sl, RT_WIDTH + h * HEAD_DIM:RT_WIDTH + (h + 1) * HEAD_DIM])
            vb = _bf(x_ref[sl, 2 * RT_WIDTH + h * HEAD_DIM:2 * RT_WIDTH + (h + 1) * HEAD_DIM])
            dmat = jnp.exp(jnp.where(col <= row, dist * log_gamma, -jnp.inf))
            att = _dot_nt(_bf(q), _bf(k)) * dmat
            state = state_ref[h]
            o = _dot(_bf(att), vb) + _dot(_bf(q * jnp.exp((pos + 1.0) * log_gamma)), _bf(state))
            k_dec = k * jnp.exp((n - 1.0 - pos) * log_gamma)
            state_ref[h] = state * math.exp(n * log_gamma) + _dot_tn(_bf(k_dec), vb)
            gate = x_ref[sl, 3 * RT_WIDTH + h * HEAD_DIM:3 * RT_WIDTH + (h + 1) * HEAD_DIM]
            o_ref[sl, col0 + h * HEAD_DIM:col0 + (h + 1) * HEAD_DIM] = _head_norm_gate(o, gain_ref[...], gate)
            yield


def _mixer_kernel(qkv_ref, hg_ref, rt_ref, gate_ref, small_ref, alog_ref, dtb_ref, dn_gain_ref, lbt_ref,
                  hg_gain_ref, cos_ref, sin_ref, rt_gain_ref, o_ref, dn_state, hg_state, rt_state,
                  *, blk, layer, group):
    @pl.when(pl.program_id(1) == 0)
    def _():
        dn_state[...] = jnp.zeros_like(dn_state)
        hg_state[...] = jnp.zeros_like(hg_state)
        rt_state[...] = jnp.zeros_like(rt_state)

    live = [
        _dn_body(qkv_ref, gate_ref, small_ref, alog_ref, dtb_ref, dn_gain_ref, o_ref, dn_state,
                 blk=blk, group=group, col0=0),
        _hg_body(hg_ref, lbt_ref, hg_gain_ref, o_ref, hg_state, blk=blk, layer=layer, col0=DN_WIDTH),
        _rt_body(rt_ref, cos_ref, sin_ref, rt_gain_ref, o_ref, rt_state, blk=blk, col0=DN_WIDTH + HG_WIDTH),
    ]
    while live:
        for body in list(live):
            if next(body, "done") == "done":
                live.remove(body)


def _mixer(qkv, rest, alog_row, dtb_row, dn_gain, lb_table, hg_gain, cos_tab, sin_tab, rt_gain,
           *, layer, batch, seq, blk=512, group=4):
    blk = min(blk, seq)
    nblk = seq // blk
    t = batch * seq
    row = lambda b, s: (b * nblk + s)
    head_row = pl.BlockSpec((None, 1, HEAD_DIM), lambda b, s: (layer, 0, 0))
    lane_row = pl.BlockSpec((None, 1, LANES), lambda b, s: (layer, 0, 0))
    state = pltpu.VMEM((DN_HEADS, HEAD_DIM, HEAD_DIM), F32)
    return pl.pallas_call(
        functools.partial(_mixer_kernel, blk=blk, layer=layer, group=group),
        grid=(batch, nblk),
        in_specs=[
            pl.BlockSpec((blk, QKV_WIDTH), lambda b, s: (row(b, s), 0)),
            pl.BlockSpec((blk, 4 * HG_WIDTH), lambda b, s: (row(b, s), 0)),
            pl.BlockSpec((blk, 4 * RT_WIDTH), lambda b, s: (row(b, s), 1)),
            pl.BlockSpec((blk, DN_WIDTH), lambda b, s: (row(b, s), REST_GATE_BLOCK)),
            pl.BlockSpec((blk, LANES), lambda b, s: (row(b, s), REST_SMALL_BLOCK)),
            lane_row, lane_row, head_row,
            pl.BlockSpec((DEPTH, HG_WIDTH), lambda b, s: (0, 0)),
            head_row,
            pl.BlockSpec((blk, HEAD_DIM), lambda b, s: (s, 0)),
            pl.BlockSpec((blk, HEAD_DIM), lambda b, s: (s, 0)),
            head_row,
        ],
        out_specs=pl.BlockSpec((blk, D_MODEL), lambda b, s: (row(b, s), 0)),
        out_shape=jax.ShapeDtypeStruct((t, D_MODEL), BF16),
        scratch_shapes=[state, pltpu.VMEM((HG_HEADS, HEAD_DIM, HEAD_DIM), F32),
                        pltpu.VMEM((RT_HEADS, HEAD_DIM, HEAD_DIM), F32)],
        compiler_params=_params("parallel", "arbitrary"),
        name="mixer",
    )(qkv, rest, rest, rest, rest, alog_row, dtb_row, dn_gain, lb_table, hg_gain, cos_tab, sin_tab, rt_gain)


def _outproj_kernel(h_ref, x_ref, w_ref, o_ref):
    o_ref[...] = h_ref[...] + _dot(x_ref[...], w_ref[...])


def _outproj(h, mixed, w, *, layer, tm=512):
    t, d = h.shape
    tm = min(tm, t)
    return pl.pallas_call(
        _outproj_kernel,
        grid=(t // tm,),
        in_specs=[
            pl.BlockSpec((tm, d), lambda i: (i, 0)),
            pl.BlockSpec((tm, d), lambda i: (i, 0)),
            pl.BlockSpec((None, d, d), lambda i: (layer, 0, 0)),
        ],
        out_specs=pl.BlockSpec((tm, d), lambda i: (i, 0)),
        out_shape=jax.ShapeDtypeStruct((t, d), F32),
        compiler_params=_params("parallel"),
        name="outproj",
    )(h, mixed, w)


def _ple_kernel(h_ref, nw_ref, wg_ref, p_ref, wp_ref, fn_ref, o_ref, *, final):
    x = h_ref[...]
    gate = _sigmoid(_dot(_bf(_rms_norm(x, nw_ref[...])), wg_ref[...]))
    y = x + gate * _dot(_bf(p_ref[...]), wp_ref[...])
    if final:
        y = _rms_norm(y, fn_ref[...])
    o_ref[...] = y


def _ple(h, nw, wg, p, wp, fn, *, layer, final, tm=512):
    t, d = h.shape
    tm = min(tm, t)
    return pl.pallas_call(
        functools.partial(_ple_kernel, final=final),
        grid=(t // tm,),
        in_specs=[
            pl.BlockSpec((tm, d), lambda i: (i, 0)),
            pl.BlockSpec((None, 1, d), lambda i: (layer, 0, 0)),
            pl.BlockSpec((None, d, d), lambda i: (layer, 0, 0)),
            pl.BlockSpec((None, tm, PLE_DIM), lambda i: (layer, i, 0)),
            pl.BlockSpec((None, PLE_DIM, d), lambda i: (layer, 0, 0)),
            pl.BlockSpec((1, d), lambda i: (0, 0)),
        ],
        out_specs=pl.BlockSpec((tm, d), lambda i: (i, 0)),
        out_shape=jax.ShapeDtypeStruct((t, d), F32),
        compiler_params=_params("parallel"),
        name="ple",
    )(h, nw, wg, p, wp, fn)


def _rest_columns(w):
    c0 = QKV_WIDTH
    c1 = c0 + DN_WIDTH
    c2 = c1 + 2 * DN_HEADS
    pad = jnp.zeros(w.shape[:2] + (REST_WIDTH - REST_MAIN - 2 * DN_HEADS,), w.dtype)
    return jnp.concatenate([w[..., c2:], w[..., c0:c1], w[..., c1:c2], pad], axis=-1)


def _lane_rows(x, offset):
    n = x.shape[1]
    return jnp.pad(x.astype(F32), ((0, 0), (offset, LANES - offset - n)))[:, None, :]


def kernel(x, p, ffn1_norm, ffn1_w_gate_up, ffn1_w_down, mix_norm, w_in, dn_conv, dn_a_log, dn_dt_bias, dn_out_norm, hg_lower_bounds, hg_out_norm, rt_out_norm, w_out, ffn2_norm, ffn2_w_gate_up, ffn2_w_down, ple_norm, ple_w_gate, ple_w_proj, final_norm):
    batch, seq, d = x.shape
    t = batch * seq
    h = x.reshape(t, d)
    cos_tab, sin_tab = _rope_tables(seq)
    rows = lambda v: v.astype(F32)[:, None, :]
    ffn1_wgu, ffn1_wd = _bf(ffn1_w_gate_up), _bf(ffn1_w_down)
    ffn2_wgu, ffn2_wd = _bf(ffn2_w_gate_up), _bf(ffn2_w_down)
    w_qkv = _bf(w_in[..., :QKV_WIDTH])
    w_rest = _bf(_rest_columns(w_in))
    w_out_bf, ple_wg, ple_wp = _bf(w_out), _bf(ple_w_gate), _bf(ple_w_proj)
    ffn1_nw, ffn2_nw, mix_nw, ple_nw = rows(ffn1_norm), rows(ffn2_norm), rows(mix_norm), rows(ple_norm)
    dn_gain, hg_gain, rt_gain = rows(dn_out_norm), rows(hg_out_norm), rows(rt_out_norm)
    alog_rows, dtb_rows = _lane_rows(dn_a_log, DN_HEADS), _lane_rows(dn_dt_bias, DN_HEADS)
    p_tok = p.reshape(DEPTH, t, PLE_DIM)
    final_nw = final_norm.astype(F32)[None, :]
    for i in range(DEPTH):
        h = _ffn(h, ffn1_nw, ffn1_wgu, ffn1_wd, layer=i)
        qkv = _qkvproj(h, mix_nw, w_qkv, dn_conv, layer=i, seq=seq)
        rest = _restproj(h, mix_nw, w_rest, layer=i)
        mixed = _mixer(qkv, rest, alog_rows, dtb_rows, dn_gain, hg_lower_bounds, hg_gain, cos_tab, sin_tab, rt_gain,
                       layer=i, batch=batch, seq=seq)
        h = _outproj(h, mixed, w_out_bf, layer=i)
        h = _ffn(h, ffn2_nw, ffn2_wgu, ffn2_wd, layer=i)
        h = _ple(h, ple_nw, ple_wg, p_tok, ple_wp, final_nw, layer=i, final=(i == DEPTH - 1))
    return h.reshape(batch, seq, d)
```

```python
import functools
import math

import jax
import jax.numpy as jnp
from jax import lax
from jax.experimental import pallas as pl
from jax.experimental.pallas import tpu as pltpu

F32 = jnp.float32
BF16 = jnp.bfloat16

D_MODEL = 2048
D_FF = 5632
DEPTH = 2
DN_HEADS = 8
HG_HEADS = 4
RT_HEADS = 4
HEAD_DIM = 128
DN_WIDTH = DN_HEADS * HEAD_DIM
HG_WIDTH = HG_HEADS * HEAD_DIM
RT_WIDTH = RT_HEADS * HEAD_DIM
QKV_WIDTH = 3 * DN_WIDTH
CONV_WIDTH = 4
PLE_DIM = 256
ROPE_THETA = 10000.0
NORM_EPS = 1e-6
LOG2_E = math.log2(math.e)

LANES = 128
SUBLANES = 8
VMEM_LIMIT = 58 * 1024 * 1024

REST_MAIN = 4 * HG_WIDTH + 4 * RT_WIDTH + DN_WIDTH
REST_WIDTH = REST_MAIN + 256
REST_GATE_BLOCK = (4 * HG_WIDTH + 4 * RT_WIDTH) // DN_WIDTH
REST_SMALL_BLOCK = REST_MAIN // LANES

CHUNK = 128
HG_LEVELS = (64, 32, 16, 8, 4, 2, 1)


def _dot(a, b):
    return jnp.dot(a, b, preferred_element_type=F32)


def _dot_nt(a, b):
    return lax.dot_general(a, b, (((1,), (1,)), ((), ())), preferred_element_type=F32)


def _dot_tn(a, b):
    return lax.dot_general(a, b, (((0,), (0,)), ((), ())), preferred_element_type=F32)


def _bf(x):
    return x.astype(BF16)


def _sigmoid(x):
    return 1.0 / (1.0 + jnp.exp(-x))


def _silu(x):
    return x * _sigmoid(x)


def _rms_norm(x, w):
    return x * lax.rsqrt(jnp.mean(x * x, axis=-1, keepdims=True) + NORM_EPS) * w


def _neg_abs(x):
    bits = lax.bitcast_convert_type(x, jnp.uint32) | jnp.uint32(0x80000000)
    return lax.bitcast_convert_type(bits, F32)


def _split_bf16(x):
    hi = x.astype(BF16)
    lo = (x - hi.astype(F32)).astype(BF16)
    return hi, lo


def _params(*sem):
    return pltpu.CompilerParams(dimension_semantics=sem, vmem_limit_bytes=VMEM_LIMIT)


def _ffn_kernel(h_ref, nw_ref, wg_ref, wu_ref, wd_ref, o_ref, hn_ref, *, nf):
    j = pl.program_id(1)

    def partial_sum():
        hn = hn_ref[...]
        g = _dot(hn, wg_ref[...])
        u = _dot(hn, wu_ref[...])
        return _dot(_bf(_silu(g) * u), wd_ref[...])

    @pl.when(j == 0)
    def _():
        hn_ref[...] = _bf(_rms_norm(h_ref[...], nw_ref[...]))
        o_ref[...] = partial_sum()

    @pl.when((j > 0) & (j < nf - 1))
    def _():
        o_ref[...] += partial_sum()

    @pl.when(j == nf - 1)
    def _():
        o_ref[...] = h_ref[...] + 0.5 * (o_ref[...] + partial_sum())


def _ffn(h, nw, wgu, wd, *, layer, tm=1024, tf=512):
    t, d = h.shape
    tm = min(tm, t)
    nf = D_FF // tf
    return pl.pallas_call(
        functools.partial(_ffn_kernel, nf=nf),
        grid=(t // tm, nf),
        in_specs=[
            pl.BlockSpec((tm, d), lambda i, j: (i, 0)),
            pl.BlockSpec((None, 1, d), lambda i, j: (layer, 0, 0)),
            pl.BlockSpec((None, d, tf), lambda i, j: (layer, 0, j)),
            pl.BlockSpec((None, d, tf), lambda i, j: (layer, 0, j + nf)),
            pl.BlockSpec((None, tf, d), lambda i, j: (layer, j, 0)),
        ],
        out_specs=pl.BlockSpec((tm, d), lambda i, j: (i, 0)),
        out_shape=jax.ShapeDtypeStruct((t, d), F32),
        scratch_shapes=[pltpu.VMEM((tm, d), BF16)],
        compiler_params=_params("parallel", "arbitrary"),
        name="ffn",
    )(h, nw, wgu, wgu, wd)


def _qkv_kernel(h_ref, nw_ref, w_ref, cw_ref, o_ref, hn_ref, tail_ref, *, tiles_per_seq, cb):
    i = pl.program_id(0)
    first = (i % tiles_per_seq) == 0
    tm = h_ref.shape[0]
    hn_ref[...] = _bf(_rms_norm(h_ref[...], nw_ref[...]))
    nchunk = QKV_WIDTH // cb
    y_next = _dot(hn_ref[...], w_ref[:, 0:cb])
    for c in range(nchunk):
        cs = slice(c * cb, (c + 1) * cb)
        y = y_next
        if c + 1 < nchunk:
            y_next = _dot(hn_ref[...], w_ref[:, (c + 1) * cb:(c + 2) * cb])
        tail = jnp.where(first, 0.0, tail_ref[:, cs])
        tail_ref[:, cs] = y[tm - SUBLANES:, :]
        z = jnp.concatenate([tail, y], axis=0)
        acc = y * cw_ref[CONV_WIDTH - 1:CONV_WIDTH, cs]
        for j in range(CONV_WIDTH - 1):
            shifted = pltpu.roll(z, CONV_WIDTH - 1 - j, axis=0)[SUBLANES:, :]
            acc = acc + shifted * cw_ref[j:j + 1, cs]
        x = _silu(acc)
        for hh in range(cb // HEAD_DIM):
            col = c * cb + hh * HEAD_DIM
            xh = x[:, hh * HEAD_DIM:(hh + 1) * HEAD_DIM]
            if col < 2 * DN_WIDTH:
                r = lax.rsqrt(jnp.sum(xh * xh, axis=-1, keepdims=True) + NORM_EPS)
                if col < DN_WIDTH:
                    r = r * (HEAD_DIM ** -0.5)
                xh = xh * r
            o_ref[:, col:col + HEAD_DIM] = xh
        zero = jnp.minimum(jnp.abs(xh[0:2 * SUBLANES, :]), 0.0)
        hn_ref[0:2 * SUBLANES, 0:HEAD_DIM] += _bf(zero)


def _qkvproj(h, nw, w, conv_w, *, layer, seq, tm=512, cb=512):
    t, d = h.shape
    tm = min(tm, seq)
    return pl.pallas_call(
        functools.partial(_qkv_kernel, tiles_per_seq=seq // tm, cb=cb),
        grid=(t // tm,),
        in_specs=[
            pl.BlockSpec((tm, d), lambda i: (i, 0)),
            pl.BlockSpec((None, 1, d), lambda i: (layer, 0, 0)),
            pl.BlockSpec((None, d, QKV_WIDTH), lambda i: (layer, 0, 0)),
            pl.BlockSpec((None, CONV_WIDTH, QKV_WIDTH), lambda i: (layer, 0, 0)),
        ],
        out_specs=pl.BlockSpec((tm, QKV_WIDTH), lambda i: (i, 0)),
        out_shape=jax.ShapeDtypeStruct((t, QKV_WIDTH), F32),
        scratch_shapes=[pltpu.VMEM((tm, d), BF16), pltpu.VMEM((SUBLANES, QKV_WIDTH), F32)],
        compiler_params=_params("arbitrary"),
        name="qkvproj",
    )(h, nw, w, conv_w)


def _restproj_kernel(h_ref, nw_ref, w_ref, o_ref, hn_ref):
    @pl.when(pl.program_id(1) == 0)
    def _():
        hn_ref[...] = _bf(_rms_norm(h_ref[...], nw_ref[...]))

    o_ref[...] = _dot(hn_ref[...], w_ref[...])


def _restproj(h, nw, w, *, layer, tm=1024, tn=1792):
    t, d = h.shape
    tm = min(tm, t)
    n = w.shape[-1]
    return pl.pallas_call(
        _restproj_kernel,
        grid=(t // tm, n // tn),
        in_specs=[
            pl.BlockSpec((tm, d), lambda i, j: (i, 0)),
            pl.BlockSpec((None, 1, d), lambda i, j: (layer, 0, 0)),
            pl.BlockSpec((None, d, tn), lambda i, j: (layer, 0, j)),
        ],
        out_specs=pl.BlockSpec((tm, tn), lambda i, j: (i, j)),
        out_shape=jax.ShapeDtypeStruct((t, n), F32),
        scratch_shapes=[pltpu.VMEM((tm, d), BF16)],
        compiler_params=_params("parallel", "arbitrary"),
        name="restproj",
    )(h, nw, w)


def _chunk_tril_bf16(blk):
    r = lax.broadcasted_iota(jnp.int32, (blk, blk), 0)
    c = lax.broadcasted_iota(jnp.int32, (blk, blk), 1)
    return jnp.where((r // CHUNK == c // CHUNK) & (c <= r), 1.0, 0.0).astype(BF16)


def _cat1(*xs):
    return jnp.concatenate(xs, axis=1)


def _bd(a, b):
    z = jnp.zeros_like(a)
    return jnp.concatenate([_cat1(a, z), _cat1(z, b)], axis=0)


def _head_norm_gate(o, gain, gate):
    o = o * lax.rsqrt(jnp.mean(o * o, axis=-1, keepdims=True) + NORM_EPS) * gain
    return _bf(o * _silu(gate))


def _dn_body(qkv_ref, gate_ref, small_ref, alog_ref, dtb_ref, gain_ref, tril_ref, o_ref, state_ref,
             *, blk, group, col0):
    nc = blk // CHUNK
    half = CHUNK // 2

    small = small_ref[...]
    beta_all = _sigmoid(small)
    x = small + dtb_ref[...]
    softplus = jnp.maximum(x, 0.0) + jnp.log(1.0 + jnp.exp(-jnp.abs(x)))
    g_all = -(jnp.exp(alog_ref[...]) * softplus) * LOG2_E
    tril = tril_ref[...]
    g_hi, g_lo = _split_bf16(g_all)
    gc_all = _dot(tril, g_hi) + _dot(tril, g_lo)
    gc_t = gc_all.T

    row = lax.broadcasted_iota(jnp.int32, (CHUNK, CHUNK), 0)
    col = lax.broadcasted_iota(jnp.int32, (CHUNK, CHUNK), 1)
    incl = col <= row
    same_half = (row // half) == (col // half)
    diag_blocks = same_half & (col < row)
    off_block = (row >= half) & (col < half)

    for g0 in range(0, DN_HEADS, group):
        heads = list(range(g0, g0 + group))
        pairs = [(heads[i], heads[i + 1]) for i in range(0, group, 2)]

        hd = {}
        for h in heads:
            q = qkv_ref[:, h * HEAD_DIM:(h + 1) * HEAD_DIM]
            k = qkv_ref[:, DN_WIDTH + h * HEAD_DIM:DN_WIDTH + (h + 1) * HEAD_DIM]
            v = qkv_ref[:, 2 * DN_WIDTH + h * HEAD_DIM:2 * DN_WIDTH + (h + 1) * HEAD_DIM]
            beta = jnp.broadcast_to(beta_all[:, h:h + 1], (blk, HEAD_DIM))
            gc = jnp.broadcast_to(gc_all[:, DN_HEADS + h:DN_HEADS + h + 1], (blk, HEAD_DIM))
            eg = jnp.exp2(gc)
            kbn = -(k * beta)
            hd[h] = dict(k=k, gc=gc, kb=_bf(k), lhs=(_bf(kbn), _bf(q)),
                         rhs=_bf(jnp.concatenate([v * beta, kbn * eg], axis=1)),
                         qeg=_bf(q * eg), gc_row=gc_t[DN_HEADS + h:DN_HEADS + h + 1, :])
        yield

        ln_bd, ln_off, qk, k_dec = {}, {}, {}, {}
        for c in range(nc):
            sl = slice(c * CHUNK, (c + 1) * CHUNK)
            for pr in pairs:
                lhs = _cat1(*[jnp.concatenate([hd[h]["lhs"][0][sl], hd[h]["lhs"][1][sl]], axis=0) for h in pr])
                a2 = _dot_nt(lhs, _bd(hd[pr[0]]["kb"][sl], hd[pr[1]]["kb"][sl]))
                for i, h in enumerate(pr):
                    d = hd[h]
                    a = a2[:, i * CHUNK:(i + 1) * CHUNK]
                    gcc = d["gc"][sl]
                    decay = jnp.exp2(jnp.where(incl, gcc - d["gc_row"][:, sl], -jnp.inf))
                    ln = a[:CHUNK] * decay
                    ln_bd[h, c] = jnp.where(diag_blocks, ln, 0.0)
                    ln_off[h, c] = jnp.where(off_block, ln, 0.0)
                    qk[h, c] = _bf(a[CHUNK:] * decay)
                    g_last = gcc[CHUNK - 1:CHUNK, :]
                    k_dec[h, c] = _bf(d["k"][sl] * jnp.exp2(g_last - gcc))
            yield

        units = [(pr, c) for c in range(nc) for pr in pairs]

        def pair_dot(xs, ws):
            y = _dot(_cat1(*xs), _bd(*ws))
            n = ws[0].shape[1]
            return y[:, :n], y[:, n:]

        p = dict(ln_bd)
        m = {}
        for (pr, c) in units:
            bs = [_bf(ln_bd[h, c]) for h in pr]
            for h, y in zip(pr, pair_dot(bs, bs)):
                m[h, c] = y
        yield
        for _ in range(int(math.log2(half)) - 2):
            for (pr, c) in units:
                mbs = [_bf(m[h, c]) for h in pr]
                xs = [jnp.concatenate([mb, _bf(p[h, c])], axis=0) for mb, h in zip(mbs, pr)]
                for h, x2 in zip(pr, pair_dot(xs, mbs)):
                    p[h, c] = p[h, c] + m[h, c] + x2[CHUNK:]
                    m[h, c] = x2[:CHUNK]
            yield
        for (pr, c) in units:
            ys = pair_dot([_bf(p[h, c]) for h in pr], [_bf(m[h, c]) for h in pr])
            for h, y in zip(pr, ys):
                p[h, c] = p[h, c] + m[h, c] + y
        yield
        for (pr, c) in units:
            pbs = [_bf(p[h, c]) for h in pr]
            ys = pair_dot([_bf(ln_off[h, c]) for h in pr], pbs)
            ys = [ln_off[h, c] + y for h, y in zip(pr, ys)]
            zs = pair_dot(pbs, [_bf(y) for y in ys])
            for h, y, z in zip(pr, ys, zs):
                p[h, c] = p[h, c] + y + z
        yield
        sol = {}
        for c in range(nc):
            for h in heads:
                rhs = hd[h]["rhs"][c * CHUNK:(c + 1) * CHUNK]
                sol[h, c] = rhs.astype(F32) + _dot(_bf(p[h, c]), rhs)
        yield

        states = {h: state_ref[h] for h in heads}
        outs = {h: [] for h in heads}
        for c in range(nc):
            sl = slice(c * CHUNK, (c + 1) * CHUNK)
            for pr in pairs:
                xs = [jnp.concatenate([_bf(sol[h, c][:, HEAD_DIM:]), hd[h]["qeg"][sl]], axis=0) for h in pr]
                wss = pair_dot(xs, [_bf(states[h]) for h in pr])
                v_new = [_bf(sol[h, c][:, :HEAD_DIM] + ws[:CHUNK]) for h, ws in zip(pr, wss)]
                intra = pair_dot([qk[h, c] for h in pr], v_new)
                for h, ws, vn, it in zip(pr, wss, v_new, intra):
                    outs[h].append(ws[CHUNK:] + it)
                    g_last = hd[h]["gc"][(c + 1) * CHUNK - 1:(c + 1) * CHUNK, :]
                    states[h] = states[h] * jnp.exp2(g_last) + _dot_tn(k_dec[h, c], vn)
            yield
        for h in heads:
            state_ref[h] = states[h]
            o = jnp.concatenate(outs[h], axis=0) if nc > 1 else outs[h][0]
            cols = slice(h * HEAD_DIM, (h + 1) * HEAD_DIM)
            o_ref[:, col0 + h * HEAD_DIM:col0 + (h + 1) * HEAD_DIM] = _head_norm_gate(
                o, gain_ref[...], gate_ref[:, cols])


def _hg_level_matrix():
    r = lax.broadcasted_iota(jnp.int32, (CHUNK, CHUNK), 0)
    c = lax.broadcasted_iota(jnp.int32, (CHUNK, CHUNK), 1)
    mats = [c <= r]
    for m in HG_LEVELS:
        mats.append(c <= (r // (2 * m)) * (2 * m) + m - 1)
    mat = jnp.concatenate([jnp.where(x, 1.0, 0.0).astype(BF16) for x in mats], axis=0)
    return _cat1(mat, mat)


def _hg_body(x_ref, lbt_ref, gain_ref, level_ref, o_ref, state_ref, *, blk, layer, col0):
    tbl = lbt_ref[...]
    e = jnp.exp(tbl - jnp.max(tbl, axis=0, keepdims=True))
    sm = e / jnp.sum(e, axis=0, keepdims=True)
    lb = jnp.zeros((1, HG_WIDTH), F32)
    for l in range(1, layer + 1):
        lb = lb + sm[l:l + 1, :]

    level_mat = level_ref[...]
    row = lax.broadcasted_iota(jnp.int32, (CHUNK, CHUNK), 0)
    col = lax.broadcasted_iota(jnp.int32, (CHUNK, CHUNK), 1)
    rowv = lax.broadcasted_iota(jnp.int32, (CHUNK, HEAD_DIM), 0)
    eye = row == col
    pair_mask = [((row // (2 * m)) == (col // (2 * m))) & ((row // m) % 2 == 1) & ((col // m) % 2 == 0)
                 for m in HG_LEVELS]
    lower_rows = [(rowv // m) % 2 == 1 for m in HG_LEVELS]
    pairs = [(h, h + 1) for h in range(0, HG_HEADS, 2)]

    for c in range(blk // CHUNK):
        sl = slice(c * CHUNK, (c + 1) * CHUNK)
        z = x_ref[sl, HG_WIDTH:2 * HG_WIDTH]
        sig = _sigmoid(z)
        log2_f = jnp.log(lb + (1.0 - lb) * sig) * LOG2_E
        hk_all = (1.0 - lb) * (1.0 - sig)
        lf_hi, lf_lo = _split_bf16(log2_f)
        cums = _dot(level_mat, jnp.concatenate([lf_hi, lf_lo], axis=0))
        yield
        for pr in pairs:
            q, k, v, b, state_t = {}, {}, {}, {}, {}
            for h in pr:
                cols = slice(h * HEAD_DIM, (h + 1) * HEAD_DIM)
                q[h] = _silu(x_ref[sl, cols]) * (HEAD_DIM ** -0.5)
                k[h] = hk_all[:, cols]
                v[h] = _bf(x_ref[sl, 2 * HG_WIDTH + h * HEAD_DIM:2 * HG_WIDTH + (h + 1) * HEAD_DIM])
                b[h] = cums[0:CHUNK, cols]
                state_t[h] = state_ref[h]
            scores = [_dot_nt(_cat1(*[_bf(q[h]) for h in pr]), _bd(*[_bf(k[h]) for h in pr]))]
            for li in range(len(HG_LEVELS)):
                xs = []
                for h in pr:
                    ref = cums[(li + 1) * CHUNK:(li + 2) * CHUNK, h * HEAD_DIM:(h + 1) * HEAD_DIM]
                    scale = jnp.exp2(_neg_abs(b[h] - ref))
                    xs.append(_bf(jnp.where(lower_rows[li], q[h], k[h]) * scale))
                scores.append(_dot_nt(_cat1(*xs), _bd(*xs)))
                if li % 2 == 1:
                    yield
            att = []
            for i, h in enumerate(pr):
                a = jnp.zeros((CHUNK, CHUNK), F32)
                for li in range(len(HG_LEVELS)):
                    a = jnp.where(pair_mask[li], scores[li + 1][:, i * CHUNK:(i + 1) * CHUNK], a)
                att.append(_bf(jnp.where(eye, scores[0][:, i * CHUNK:(i + 1) * CHUNK], a)))
            intra = _dot(_cat1(*att), _bd(*[v[h] for h in pr]))
            inter = _dot_nt(_cat1(*[_bf(q[h] * jnp.exp2(b[h])) for h in pr]),
                            _bd(*[_bf(state_t[h]) for h in pr]))
            o2 = intra + inter
            yield
            for i, h in enumerate(pr):
                b_last = b[h][CHUNK - 1:CHUNK, :]
                k_dec = k[h] * jnp.exp2(b_last - b[h])
                state_ref[h] = state_t[h] * jnp.exp2(b_last) + _dot_tn(v[h], _bf(k_dec))
                gate = x_ref[sl, 3 * HG_WIDTH + h * HEAD_DIM:3 * HG_WIDTH + (h + 1) * HEAD_DIM]
                o_ref[sl, col0 + h * HEAD_DIM:col0 + (h + 1) * HEAD_DIM] = _head_norm_gate(
                    o2[:, i * HEAD_DIM:(i + 1) * HEAD_DIM], gain_ref[...], gate)


RT_CHUNK = 256


def _tables_kernel(cos_ref, sin_ref, tril_ref, level_ref, dmat_ref, qdec_ref, kdec_ref):
    shape = cos_ref.shape
    half = HEAD_DIM // 2
    pos = lax.broadcasted_iota(jnp.int32, shape, 0).astype(F32)
    lane = lax.broadcasted_iota(jnp.int32, shape, 1)
    inv_freq = jnp.exp((lane % half).astype(F32) * (-math.log(ROPE_THETA) / half))
    ang = pos * inv_freq
    cos_ref[...] = jnp.cos(ang)
    sin = jnp.sin(ang)
    sin_ref[...] = jnp.where(lane < half, -sin, sin)
    tril_ref[...] = _chunk_tril_bf16(tril_ref.shape[0])
    level_ref[...] = _hg_level_matrix()
    n = RT_CHUNK
    row = lax.broadcasted_iota(jnp.int32, (n, n), 0)
    col = lax.broadcasted_iota(jnp.int32, (n, n), 1)
    dist = (row - col).astype(F32)
    p = lax.broadcasted_iota(jnp.int32, (n, HEAD_DIM), 0).astype(F32)
    for h in range(RT_HEADS):
        log_gamma = math.log(1.0 - 2.0 ** (-5.0 - h))
        dmat_ref[h] = jnp.exp(jnp.where(col <= row, dist * log_gamma, -jnp.inf))
        qdec_ref[h] = jnp.exp((p + 1.0) * log_gamma)
        kdec_ref[h] = jnp.exp((n - 1.0 - p) * log_gamma)


def _tables(seq, blk):
    f32 = lambda *shape: jax.ShapeDtypeStruct(shape, F32)
    out_shape = (f32(seq, HEAD_DIM), f32(seq, HEAD_DIM),
                 jax.ShapeDtypeStruct((blk, blk), BF16),
                 jax.ShapeDtypeStruct(((len(HG_LEVELS) + 1) * CHUNK, 2 * CHUNK), BF16),
                 f32(RT_HEADS, RT_CHUNK, RT_CHUNK), f32(RT_HEADS, RT_CHUNK, HEAD_DIM),
                 f32(RT_HEADS, RT_CHUNK, HEAD_DIM))
    return pl.pallas_call(_tables_kernel, out_shape=out_shape, name="tables")()


def _rt_body(x_ref, cos_ref, sin_ref, dmat_ref, qdec_ref, kdec_ref, gain_ref, o_ref, state_ref, *, blk, col0):
    n = RT_CHUNK

    for c in range(blk // n):
        sl = slice(c * n, (c + 1) * n)
        cos = cos_ref[sl, :]
        sin = sin_ref[sl, :]

        def rotary(x):
            return x * cos + pltpu.roll(x, HEAD_DIM // 2, axis=1) * sin

        for h in range(RT_HEADS):
            log_gamma = math.log(1.0 - 2.0 ** (-5.0 - h))
            cols = slice(h * HEAD_DIM, (h + 1) * HEAD_DIM)
            q = rotary(x_ref[sl, cols]) * (HEAD_DIM ** -0.5)
            k = rotary(x_ref[sl, RT_WIDTH + h * HEAD_DIM:RT_WIDTH + (h + 1) * HEAD_DIM])
            vb = _bf(x_ref[sl, 2 * RT_WIDTH + h * HEAD_DIM:2 * RT_WIDTH + (h + 1) * HEAD_DIM])
            att = _dot_nt(_bf(q), _bf(k)) * dmat_ref[h]
            state = state_ref[h]
            o = _dot(_bf(att), vb) + _dot(_bf(q * qdec_ref[h]), _bf(state))
            k_dec = k * kdec_ref[h]
            state_ref[h] = state * math.exp(n * log_gamma) + _dot_tn(_bf(k_dec), vb)
            gate = x_ref[sl, 3 * RT_WIDTH + h * HEAD_DIM:3 * RT_WIDTH + (h + 1) * HEAD_DIM]
            o_ref[sl, col0 + h * HEAD_DIM:col0 + (h + 1) * HEAD_DIM] = _head_norm_gate(o, gain_ref[...], gate)
            yield


def _mixer_kernel(qkv_ref, hg_ref, rt_ref, gate_ref, small_ref, alog_ref, dtb_ref, dn_gain_ref, lbt_ref,
                  hg_gain_ref, cos_ref, sin_ref, rt_gain_ref, tril_ref, level_ref, dmat_ref, qdec_ref, kdec_ref,
                  o_ref, dn_state, hg_state, rt_state, *, blk, layer, group):
    @pl.when(pl.program_id(1) == 0)
    def _():
        dn_state[...] = jnp.zeros_like(dn_state)
        hg_state[...] = jnp.zeros_like(hg_state)
        rt_state[...] = jnp.zeros_like(rt_state)

    live = [
        _dn_body(qkv_ref, gate_ref, small_ref, alog_ref, dtb_ref, dn_gain_ref, tril_ref, o_ref, dn_state,
                 blk=blk, group=group, col0=0),
        _hg_body(hg_ref, lbt_ref, hg_gain_ref, level_ref, o_ref, hg_state, blk=blk, layer=layer, col0=DN_WIDTH),
        _rt_body(rt_ref, cos_ref, sin_ref, dmat_ref, qdec_ref, kdec_ref, rt_gain_ref, o_ref, rt_state,
                 blk=blk, col0=DN_WIDTH + HG_WIDTH),
    ]
    while live:
        for body in list(live):
            if next(body, "done") == "done":
                live.remove(body)


MIXER_BLOCK = 512


def _mixer(qkv, rest, alog_row, dtb_row, dn_gain, lb_table, hg_gain, tables, rt_gain,
           *, layer, batch, seq, blk, group=4):
    cos_tab, sin_tab, tril, level_mat, rt_dmat, rt_qdec, rt_kdec = tables
    nblk = seq // blk
    t = batch * seq
    row = lambda b, s: (b * nblk + s)
    head_row = pl.BlockSpec((None, 1, HEAD_DIM), lambda b, s: (layer, 0, 0))
    lane_row = pl.BlockSpec((None, 1, LANES), lambda b, s: (layer, 0, 0))
    state = pltpu.VMEM((DN_HEADS, HEAD_DIM, HEAD_DIM), F32)
    return pl.pallas_call(
        functools.partial(_mixer_kernel, blk=blk, layer=layer, group=group),
        grid=(batch, nblk),
        in_specs=[
            pl.BlockSpec((blk, QKV_WIDTH), lambda b, s: (row(b, s), 0)),
            pl.BlockSpec((blk, 4 * HG_WIDTH), lambda b, s: (row(b, s), 0)),
            pl.BlockSpec((blk, 4 * RT_WIDTH), lambda b, s: (row(b, s), 1)),
            pl.BlockSpec((blk, DN_WIDTH), lambda b, s: (row(b, s), REST_GATE_BLOCK)),
            pl.BlockSpec((blk, LANES), lambda b, s: (row(b, s), REST_SMALL_BLOCK)),
            lane_row, lane_row, head_row,
            pl.BlockSpec((DEPTH, HG_WIDTH), lambda b, s: (0, 0)),
            head_row,
            pl.BlockSpec((blk, HEAD_DIM), lambda b, s: (s, 0)),
            pl.BlockSpec((blk, HEAD_DIM), lambda b, s: (s, 0)),
            head_row,
            pl.BlockSpec(tril.shape, lambda b, s: (0, 0)),
            pl.BlockSpec(level_mat.shape, lambda b, s: (0, 0)),
            pl.BlockSpec(rt_dmat.shape, lambda b, s: (0, 0, 0)),
            pl.BlockSpec(rt_qdec.shape, lambda b, s: (0, 0, 0)),
            pl.BlockSpec(rt_kdec.shape, lambda b, s: (0, 0, 0)),
        ],
        out_specs=pl.BlockSpec((blk, D_MODEL), lambda b, s: (row(b, s), 0)),
        out_shape=jax.ShapeDtypeStruct((t, D_MODEL), BF16),
        scratch_shapes=[state, pltpu.VMEM((HG_HEADS, HEAD_DIM, HEAD_DIM), F32),
                        pltpu.VMEM((RT_HEADS, HEAD_DIM, HEAD_DIM), F32)],
        compiler_params=_params("parallel", "arbitrary"),
        name="mixer",
    )(qkv, rest, rest, rest, rest, alog_row, dtb_row, dn_gain, lb_table, hg_gain, cos_tab, sin_tab, rt_gain,
      tril, level_mat, rt_dmat, rt_qdec, rt_kdec)


def _outproj_kernel(h_ref, x_ref, w_ref, o_ref):
    o_ref[...] = h_ref[...] + _dot(x_ref[...], w_ref[...])


def _outproj(h, mixed, w, *, layer, tm=512):
    t, d = h.shape
    tm = min(tm, t)
    return pl.pallas_call(
        _outproj_kernel,
        grid=(t // tm,),
        in_specs=[
            pl.BlockSpec((tm, d), lambda i: (i, 0)),
            pl.BlockSpec((tm, d), lambda i: (i, 0)),
            pl.BlockSpec((None, d, d), lambda i: (layer, 0, 0)),
        ],
        out_specs=pl.BlockSpec((tm, d), lambda i: (i, 0)),
        out_shape=jax.ShapeDtypeStruct((t, d), F32),
        compiler_params=_params("parallel"),
        name="outproj",
    )(h, mixed, w)


def _ple_kernel(h_ref, nw_ref, wg_ref, p_ref, wp_ref, fn_ref, o_ref, *, final):
    x = h_ref[...]
    gate = _sigmoid(_dot(_bf(_rms_norm(x, nw_ref[...])), wg_ref[...]))
    y = x + gate * _dot(_bf(p_ref[...]), wp_ref[...])
    if final:
        y = _rms_norm(y, fn_ref[...])
    o_ref[...] = y


def _ple(h, nw, wg, p, wp, fn, *, layer, final, tm=512):
    t, d = h.shape
    tm = min(tm, t)
    return pl.pallas_call(
        functools.partial(_ple_kernel, final=final),
        grid=(t // tm,),
        in_specs=[
            pl.BlockSpec((tm, d), lambda i: (i, 0)),
            pl.BlockSpec((None, 1, d), lambda i: (layer, 0, 0)),
            pl.BlockSpec((None, d, d), lambda i: (layer, 0, 0)),
            pl.BlockSpec((None, tm, PLE_DIM), lambda i: (layer, i, 0)),
            pl.BlockSpec((None, PLE_DIM, d), lambda i: (layer, 0, 0)),
            pl.BlockSpec((1, d), lambda i: (0, 0)),
        ],
        out_specs=pl.BlockSpec((tm, d), lambda i: (i, 0)),
        out_shape=jax.ShapeDtypeStruct((t, d), F32),
        compiler_params=_params("parallel"),
        name="ple",
    )(h, nw, wg, p, wp, fn)


def _rest_columns(w):
    c0 = QKV_WIDTH
    c1 = c0 + DN_WIDTH
    c2 = c1 + 2 * DN_HEADS
    pad = jnp.zeros(w.shape[:2] + (REST_WIDTH - REST_MAIN - 2 * DN_HEADS,), w.dtype)
    return jnp.concatenate([w[..., c2:], w[..., c0:c1], w[..., c1:c2], pad], axis=-1)


def _lane_rows(x, offset):
    n = x.shape[1]
    return jnp.pad(x.astype(F32), ((0, 0), (offset, LANES - offset - n)))[:, None, :]


def kernel(x, p, ffn1_norm, ffn1_w_gate_up, ffn1_w_down, mix_norm, w_in, dn_conv, dn_a_log, dn_dt_bias, dn_out_norm, hg_lower_bounds, hg_out_norm, rt_out_norm, w_out, ffn2_norm, ffn2_w_gate_up, ffn2_w_down, ple_norm, ple_w_gate, ple_w_proj, final_norm):
    batch, seq, d = x.shape
    t = batch * seq
    h = x.reshape(t, d)
    blk = min(MIXER_BLOCK, seq)
    tables = _tables(seq, blk)
    rows = lambda v: v.astype(F32)[:, None, :]
    ffn1_wgu, ffn1_wd = _bf(ffn1_w_gate_up), _bf(ffn1_w_down)
    ffn2_wgu, ffn2_wd = _bf(ffn2_w_gate_up), _bf(ffn2_w_down)
    w_qkv = _bf(w_in[..., :QKV_WIDTH])
    w_rest = _bf(_rest_columns(w_in))
    w_out_bf, ple_wg, ple_wp = _bf(w_out), _bf(ple_w_gate), _bf(ple_w_proj)
    ffn1_nw, ffn2_nw, mix_nw, ple_nw = rows(ffn1_norm), rows(ffn2_norm), rows(mix_norm), rows(ple_norm)
    dn_gain, hg_gain, rt_gain = rows(dn_out_norm), rows(hg_out_norm), rows(rt_out_norm)
    alog_rows, dtb_rows = _lane_rows(dn_a_log, DN_HEADS), _lane_rows(dn_dt_bias, DN_HEADS)
    p_tok = p.reshape(DEPTH, t, PLE_DIM)
    final_nw = final_norm.astype(F32)[None, :]
    for i in range(DEPTH):
        h = _ffn(h, ffn1_nw, ffn1_wgu, ffn1_wd, layer=i)
        qkv = _qkvproj(h, mix_nw, w_qkv, dn_conv, layer=i, seq=seq)
        rest = _restproj(h, mix_nw, w_rest, layer=i)
        mixed = _mixer(qkv, rest, alog_rows, dtb_rows, dn_gain, hg_lower_bounds, hg_gain, tables, rt_gain,
                       layer=i, batch=batch, seq=seq, blk=blk)
        h = _outproj(h, mixed, w_out_bf, layer=i)
        h = _ffn(h, ffn2_nw, ffn2_wgu, ffn2_wd, layer=i)
        h = _ple(h, ple_nw, ple_wg, p_tok, ple_wp, final_nw, layer=i, final=(i == DEPTH - 1))
    return h.reshape(batch, seq, d)
```

```python
import functools
import math

import jax
import jax.numpy as jnp
from jax import lax
from jax.experimental import pallas as pl
from jax.experimental.pallas import tpu as pltpu

F32 = jnp.float32
BF16 = jnp.bfloat16

D_MODEL = 2048
D_FF = 5632
DEPTH = 2
DN_HEADS = 8
HG_HEADS = 4
RT_HEADS = 4
HEAD_DIM = 128
DN_WIDTH = DN_HEADS * HEAD_DIM
HG_WIDTH = HG_HEADS * HEAD_DIM
RT_WIDTH = RT_HEADS * HEAD_DIM
QKV_WIDTH = 3 * DN_WIDTH
CONV_WIDTH = 4
PLE_DIM = 256
ROPE_THETA = 10000.0
NORM_EPS = 1e-6
LOG2_E = math.log2(math.e)

LANES = 128
SUBLANES = 8
VMEM_LIMIT = 58 * 1024 * 1024

REST_MAIN = 4 * HG_WIDTH + 4 * RT_WIDTH + DN_WIDTH
REST_WIDTH = REST_MAIN + 256
REST_GATE_BLOCK = (4 * HG_WIDTH + 4 * RT_WIDTH) // DN_WIDTH
REST_SMALL_BLOCK = REST_MAIN // LANES

CHUNK = 128
HG_LEVELS = (64, 32, 16, 8, 4, 2, 1)


def _dot(a, b):
    return jnp.dot(a, b, preferred_element_type=F32)


def _dot_nt(a, b):
    return lax.dot_general(a, b, (((1,), (1,)), ((), ())), preferred_element_type=F32)


def _dot_tn(a, b):
    return lax.dot_general(a, b, (((0,), (0,)), ((), ())), preferred_element_type=F32)


def _bf(x):
    return x.astype(BF16)


def _sigmoid(x):
    return 1.0 / (1.0 + jnp.exp(-x))


def _silu(x):
    return x * _sigmoid(x)


def _rms_norm(x, w):
    return x * lax.rsqrt(jnp.mean(x * x, axis=-1, keepdims=True) + NORM_EPS) * w


def _neg_abs(x):
    bits = lax.bitcast_convert_type(x, jnp.uint32) | jnp.uint32(0x80000000)
    return lax.bitcast_convert_type(bits, F32)


def _split_bf16(x):
    hi = x.astype(BF16)
    lo = (x - hi.astype(F32)).astype(BF16)
    return hi, lo


def _params(*sem):
    return pltpu.CompilerParams(dimension_semantics=sem, vmem_limit_bytes=VMEM_LIMIT)


def _ffn_kernel(h_ref, nw_ref, wg_ref, wu_ref, wd_ref, o_ref, hn_ref, *, nf):
    j = pl.program_id(1)

    def partial_sum():
        hn = hn_ref[...]
        g = _dot(hn, wg_ref[...])
        u = _dot(hn, wu_ref[...])
        return _dot(_bf(_silu(g) * u), wd_ref[...])

    @pl.when(j == 0)
    def _():
        hn_ref[...] = _bf(_rms_norm(h_ref[...], nw_ref[...]))
        o_ref[...] = partial_sum()

    @pl.when((j > 0) & (j < nf - 1))
    def _():
        o_ref[...] += partial_sum()

    @pl.when(j == nf - 1)
    def _():
        o_ref[...] = h_ref[...] + 0.5 * (o_ref[...] + partial_sum())


def _ffn(h, nw, wgu, wd, *, layer, tm=1024, tf=512):
    t, d = h.shape
    tm = min(tm, t)
    nf = D_FF // tf
    return pl.pallas_call(
        functools.partial(_ffn_kernel, nf=nf),
        grid=(t // tm, nf),
        in_specs=[
            pl.BlockSpec((tm, d), lambda i, j: (i, 0)),
            pl.BlockSpec((None, 1, d), lambda i, j: (layer, 0, 0)),
            pl.BlockSpec((None, d, tf), lambda i, j: (layer, 0, j)),
            pl.BlockSpec((None, d, tf), lambda i, j: (layer, 0, j + nf)),
            pl.BlockSpec((None, tf, d), lambda i, j: (layer, j, 0)),
        ],
        out_specs=pl.BlockSpec((tm, d), lambda i, j: (i, 0)),
        out_shape=jax.ShapeDtypeStruct((t, d), F32),
        scratch_shapes=[pltpu.VMEM((tm, d), BF16)],
        compiler_params=_params("parallel", "arbitrary"),
        name="ffn",
    )(h, nw, wgu, wgu, wd)


def _qkv_kernel(h_ref, nw_ref, w_ref, cw_ref, o_ref, hn_ref, tail_ref, *, tiles_per_seq, cb):
    i = pl.program_id(0)
    first = (i % tiles_per_seq) == 0
    tm = h_ref.shape[0]
    hn_ref[...] = _bf(_rms_norm(h_ref[...], nw_ref[...]))
    nchunk = QKV_WIDTH // cb
    y_next = _dot(hn_ref[...], w_ref[:, 0:cb])
    for c in range(nchunk):
        cs = slice(c * cb, (c + 1) * cb)
        y = y_next
        if c + 1 < nchunk:
            y_next = _dot(hn_ref[...], w_ref[:, (c + 1) * cb:(c + 2) * cb])
        tail = jnp.where(first, 0.0, tail_ref[:, cs])
        tail_ref[:, cs] = y[tm - SUBLANES:, :]
        z = jnp.concatenate([tail, y], axis=0)
        acc = y * cw_ref[CONV_WIDTH - 1:CONV_WIDTH, cs]
        for j in range(CONV_WIDTH - 1):
            shifted = pltpu.roll(z, CONV_WIDTH - 1 - j, axis=0)[SUBLANES:, :]
            acc = acc + shifted * cw_ref[j:j + 1, cs]
        x = _silu(acc)
        for hh in range(cb // HEAD_DIM):
            col = c * cb + hh * HEAD_DIM
            xh = x[:, hh * HEAD_DIM:(hh + 1) * HEAD_DIM]
            if col < 2 * DN_WIDTH:
                r = lax.rsqrt(jnp.sum(xh * xh, axis=-1, keepdims=True) + NORM_EPS)
                if col < DN_WIDTH:
                    r = r * (HEAD_DIM ** -0.5)
                xh = xh * r
            o_ref[:, col:col + HEAD_DIM] = xh
        zero = jnp.minimum(jnp.abs(xh[0:2 * SUBLANES, :]), 0.0)
        hn_ref[0:2 * SUBLANES, 0:HEAD_DIM] += _bf(zero)


def _qkvproj(h, nw, w, conv_w, *, layer, seq, tm=512, cb=512):
    t, d = h.shape
    tm = min(tm, seq)
    return pl.pallas_call(
        functools.partial(_qkv_kernel, tiles_per_seq=seq // tm, cb=cb),
        grid=(t // tm,),
        in_specs=[
            pl.BlockSpec((tm, d), lambda i: (i, 0)),
            pl.BlockSpec((None, 1, d), lambda i: (layer, 0, 0)),
            pl.BlockSpec((None, d, QKV_WIDTH), lambda i: (layer, 0, 0)),
            pl.BlockSpec((None, CONV_WIDTH, QKV_WIDTH), lambda i: (layer, 0, 0)),
        ],
        out_specs=pl.BlockSpec((tm, QKV_WIDTH), lambda i: (i, 0)),
        out_shape=jax.ShapeDtypeStruct((t, QKV_WIDTH), F32),
        scratch_shapes=[pltpu.VMEM((tm, d), BF16), pltpu.VMEM((SUBLANES, QKV_WIDTH), F32)],
        compiler_params=_params("arbitrary"),
        name="qkvproj",
    )(h, nw, w, conv_w)


def _restproj_kernel(h_ref, nw_ref, w_ref, o_ref, hn_ref):
    @pl.when(pl.program_id(1) == 0)
    def _():
        hn_ref[...] = _bf(_rms_norm(h_ref[...], nw_ref[...]))

    o_ref[...] = _dot(hn_ref[...], w_ref[...])


def _restproj(h, nw, w, *, layer, tm=1024, tn=1792):
    t, d = h.shape
    tm = min(tm, t)
    n = w.shape[-1]
    return pl.pallas_call(
        _restproj_kernel,
        grid=(t // tm, n // tn),
        in_specs=[
            pl.BlockSpec((tm, d), lambda i, j: (i, 0)),
            pl.BlockSpec((None, 1, d), lambda i, j: (layer, 0, 0)),
            pl.BlockSpec((None, d, tn), lambda i, j: (layer, 0, j)),
        ],
        out_specs=pl.BlockSpec((tm, tn), lambda i, j: (i, j)),
        out_shape=jax.ShapeDtypeStruct((t, n), F32),
        scratch_shapes=[pltpu.VMEM((tm, d), BF16)],
        compiler_params=_params("parallel", "arbitrary"),
        name="restproj",
    )(h, nw, w)


def _chunk_tril_bf16(blk):
    r = lax.broadcasted_iota(jnp.int32, (blk, blk), 0)
    c = lax.broadcasted_iota(jnp.int32, (blk, blk), 1)
    return jnp.where((r // CHUNK == c // CHUNK) & (c <= r), 1.0, 0.0).astype(BF16)


def _cat1(*xs):
    return jnp.concatenate(xs, axis=1)


def _bd(a, b):
    z = jnp.zeros_like(a)
    return jnp.concatenate([_cat1(a, z), _cat1(z, b)], axis=0)


def _head_norm_gate(o, gain, gate):
    o = o * lax.rsqrt(jnp.mean(o * o, axis=-1, keepdims=True) + NORM_EPS) * gain
    return _bf(o * _silu(gate))


def _dn_body(qkv_ref, gate_ref, small_ref, alog_ref, dtb_ref, gain_ref, tril_ref, o_ref, state_ref,
             *, blk, group, col0):
    nc = blk // CHUNK
    half = CHUNK // 2

    small = small_ref[...]
    beta_all = _sigmoid(small)
    x = small + dtb_ref[...]
    softplus = jnp.maximum(x, 0.0) + jnp.log(1.0 + jnp.exp(-jnp.abs(x)))
    g_all = -(jnp.exp(alog_ref[...]) * softplus) * LOG2_E
    tril = tril_ref[...]
    g_hi, g_lo = _split_bf16(g_all)
    gc_all = _dot(tril, g_hi) + _dot(tril, g_lo)
    gc_t = gc_all.T

    row = lax.broadcasted_iota(jnp.int32, (CHUNK, CHUNK), 0)
    col = lax.broadcasted_iota(jnp.int32, (CHUNK, CHUNK), 1)
    incl = col <= row
    same_half = (row // half) == (col // half)
    diag_blocks = same_half & (col < row)
    off_block = (row >= half) & (col < half)

    for g0 in range(0, DN_HEADS, group):
        heads = list(range(g0, g0 + group))
        pairs = [(heads[i], heads[i + 1]) for i in range(0, group, 2)]

        hd = {}
        for h in heads:
            q = qkv_ref[:, h * HEAD_DIM:(h + 1) * HEAD_DIM]
            k = qkv_ref[:, DN_WIDTH + h * HEAD_DIM:DN_WIDTH + (h + 1) * HEAD_DIM]
            v = qkv_ref[:, 2 * DN_WIDTH + h * HEAD_DIM:2 * DN_WIDTH + (h + 1) * HEAD_DIM]
            beta = jnp.broadcast_to(beta_all[:, h:h + 1], (blk, HEAD_DIM))
            gc = jnp.broadcast_to(gc_all[:, DN_HEADS + h:DN_HEADS + h + 1], (blk, HEAD_DIM))
            eg = jnp.exp2(gc)
            kbn = -(k * beta)
            hd[h] = dict(k=k, gc=gc, kb=_bf(k), lhs=(_bf(kbn), _bf(q)),
                         rhs=_bf(jnp.concatenate([v * beta, kbn * eg], axis=1)),
                         qeg=_bf(q * eg), gc_row=gc_t[DN_HEADS + h:DN_HEADS + h + 1, :])
        yield

        ln_bd, ln_off, qk, k_dec = {}, {}, {}, {}
        for c in range(nc):
            sl = slice(c * CHUNK, (c + 1) * CHUNK)
            for pr in pairs:
                lhs = _cat1(*[jnp.concatenate([hd[h]["lhs"][0][sl], hd[h]["lhs"][1][sl]], axis=0) for h in pr])
                a2 = _dot_nt(lhs, _bd(hd[pr[0]]["kb"][sl], hd[pr[1]]["kb"][sl]))
                for i, h in enumerate(pr):
                    d = hd[h]
                    a = a2[:, i * CHUNK:(i + 1) * CHUNK]
                    gcc = d["gc"][sl]
                    decay = jnp.exp2(jnp.where(incl, gcc - d["gc_row"][:, sl], -jnp.inf))
                    ln = a[:CHUNK] * decay
                    ln_bd[h, c] = jnp.where(diag_blocks, ln, 0.0)
                    ln_off[h, c] = jnp.where(off_block, ln, 0.0)
                    qk[h, c] = _bf(a[CHUNK:] * decay)
                    g_last = gcc[CHUNK - 1:CHUNK, :]
                    k_dec[h, c] = _bf(d["k"][sl] * jnp.exp2(g_last - gcc))
            yield

        units = [(pr, c) for c in range(nc) for pr in pairs]

        def pair_dot(xs, ws):
            y = _dot(_cat1(*xs), _bd(*ws))
            n = ws[0].shape[1]
            return y[:, :n], y[:, n:]

        p = dict(ln_bd)
        m = {}
        for (pr, c) in units:
            bs = [_bf(ln_bd[h, c]) for h in pr]
            for h, y in zip(pr, pair_dot(bs, bs)):
                m[h, c] = y
        yield
        for _ in range(int(math.log2(half)) - 2):
            for (pr, c) in units:
                mbs = [_bf(m[h, c]) for h in pr]
                xs = [jnp.concatenate([mb, _bf(p[h, c])], axis=0) for mb, h in zip(mbs, pr)]
                for h, x2 in zip(pr, pair_dot(xs, mbs)):
                    p[h, c] = p[h, c] + m[h, c] + x2[CHUNK:]
                    m[h, c] = x2[:CHUNK]
            yield
        for (pr, c) in units:
            ys = pair_dot([_bf(p[h, c]) for h in pr], [_bf(m[h, c]) for h in pr])
            for h, y in zip(pr, ys):
                p[h, c] = p[h, c] + m[h, c] + y
        yield
        for (pr, c) in units:
            pbs = [_bf(p[h, c]) for h in pr]
            ys = pair_dot([_bf(ln_off[h, c]) for h in pr], pbs)
            ys = [ln_off[h, c] + y for h, y in zip(pr, ys)]
            zs = pair_dot(pbs, [_bf(y) for y in ys])
            for h, y, z in zip(pr, ys, zs):
                p[h, c] = p[h, c] + y + z
        yield
        sol = {}
        for c in range(nc):
            for h in heads:
                rhs = hd[h]["rhs"][c * CHUNK:(c + 1) * CHUNK]
                sol[h, c] = rhs.astype(F32) + _dot(_bf(p[h, c]), rhs)
        yield

        states = {h: state_ref[h] for h in heads}
        outs = {h: [] for h in heads}
        for c in range(nc):
            sl = slice(c * CHUNK, (c + 1) * CHUNK)
            for pr in pairs:
                xs = [jnp.concatenate([_bf(sol[h, c][:, HEAD_DIM:]), hd[h]["qeg"][sl]], axis=0) for h in pr]
                wss = pair_dot(xs, [_bf(states[h]) for h in pr])
                v_new = [_bf(sol[h, c][:, :HEAD_DIM] + ws[:CHUNK]) for h, ws in zip(pr, wss)]
                intra = pair_dot([qk[h, c] for h in pr], v_new)
                for h, ws, vn, it in zip(pr, wss, v_new, intra):
                    outs[h].append(ws[CHUNK:] + it)
                    g_last = hd[h]["gc"][(c + 1) * CHUNK - 1:(c + 1) * CHUNK, :]
                    states[h] = states[h] * jnp.exp2(g_last) + _dot_tn(k_dec[h, c], vn)
            yield
        for h in heads:
            state_ref[h] = states[h]
            o = jnp.concatenate(outs[h], axis=0) if nc > 1 else outs[h][0]
            cols = slice(h * HEAD_DIM, (h + 1) * HEAD_DIM)
            o_ref[:, col0 + h * HEAD_DIM:col0 + (h + 1) * HEAD_DIM] = _head_norm_gate(
                o, gain_ref[...], gate_ref[:, cols])


def _hg_level_matrix():
    r = lax.broadcasted_iota(jnp.int32, (CHUNK, CHUNK), 0)
    c = lax.broadcasted_iota(jnp.int32, (CHUNK, CHUNK), 1)
    mats = [c <= r]
    for m in HG_LEVELS:
        mats.append(c <= (r // (2 * m)) * (2 * m) + m - 1)
    mat = jnp.concatenate([jnp.where(x, 1.0, 0.0).astype(BF16) for x in mats], axis=0)
    return _cat1(mat, mat)


def _hg_body(x_ref, lbt_ref, gain_ref, level_ref, o_ref, state_ref, *, blk, layer, col0):
    tbl = lbt_ref[...]
    e = jnp.exp(tbl - jnp.max(tbl, axis=0, keepdims=True))
    sm = e / jnp.sum(e, axis=0, keepdims=True)
    lb = jnp.zeros((1, HG_WIDTH), F32)
    for l in range(1, layer + 1):
        lb = lb + sm[l:l + 1, :]

    level_mat = level_ref[...]
    row = lax.broadcasted_iota(jnp.int32, (CHUNK, CHUNK), 0)
    col = lax.broadcasted_iota(jnp.int32, (CHUNK, CHUNK), 1)
    rowv = lax.broadcasted_iota(jnp.int32, (CHUNK, HEAD_DIM), 0)
    eye = row == col
    pair_mask = [((row // (2 * m)) == (col // (2 * m))) & ((row // m) % 2 == 1) & ((col // m) % 2 == 0)
                 for m in HG_LEVELS]
    lower_rows = [(rowv // m) % 2 == 1 for m in HG_LEVELS]
    pairs = [(h, h + 1) for h in range(0, HG_HEADS, 2)]

    for c in range(blk // CHUNK):
        sl = slice(c * CHUNK, (c + 1) * CHUNK)
        z = x_ref[sl, HG_WIDTH:2 * HG_WIDTH]
        sig = _sigmoid(z)
        log2_f = jnp.log(lb + (1.0 - lb) * sig) * LOG2_E
        hk_all = (1.0 - lb) * (1.0 - sig)
        lf_hi, lf_lo = _split_bf16(log2_f)
        cums = _dot(level_mat, jnp.concatenate([lf_hi, lf_lo], axis=0))
        yield
        for pr in pairs:
            q, k, v, b, state_t = {}, {}, {}, {}, {}
            for h in pr:
                cols = slice(h * HEAD_DIM, (h + 1) * HEAD_DIM)
                q[h] = _silu(x_ref[sl, cols]) * (HEAD_DIM ** -0.5)
                k[h] = hk_all[:, cols]
                v[h] = _bf(x_ref[sl, 2 * HG_WIDTH + h * HEAD_DIM:2 * HG_WIDTH + (h + 1) * HEAD_DIM])
                b[h] = cums[0:CHUNK, cols]
                state_t[h] = state_ref[h]
            scores = [_dot_nt(_cat1(*[_bf(q[h]) for h in pr]), _bd(*[_bf(k[h]) for h in pr]))]
            for li in range(len(HG_LEVELS)):
                xs = []
                for h in pr:
                    ref = cums[(li + 1) * CHUNK:(li + 2) * CHUNK, h * HEAD_DIM:(h + 1) * HEAD_DIM]
                    scale = jnp.exp2(_neg_abs(b[h] - ref))
                    xs.append(_bf(jnp.where(lower_rows[li], q[h], k[h]) * scale))
                scores.append(_dot_nt(_cat1(*xs), _bd(*xs)))
                if li % 2 == 1:
                    yield
            att = []
            for i, h in enumerate(pr):
                a = jnp.zeros((CHUNK, CHUNK), F32)
                for li in range(len(HG_LEVELS)):
                    a = jnp.where(pair_mask[li], scores[li + 1][:, i * CHUNK:(i + 1) * CHUNK], a)
                att.append(_bf(jnp.where(eye, scores[0][:, i * CHUNK:(i + 1) * CHUNK], a)))
            intra = _dot(_cat1(*att), _bd(*[v[h] for h in pr]))
            inter = _dot_nt(_cat1(*[_bf(q[h] * jnp.exp2(b[h])) for h in pr]),
                            _bd(*[_bf(state_t[h]) for h in pr]))
            o2 = intra + inter
            yield
            for i, h in enumerate(pr):
                b_last = b[h][CHUNK - 1:CHUNK, :]
                k_dec = k[h] * jnp.exp2(b_last - b[h])
                state_ref[h] = state_t[h] * jnp.exp2(b_last) + _dot_tn(v[h], _bf(k_dec))
                gate = x_ref[sl, 3 * HG_WIDTH + h * HEAD_DIM:3 * HG_WIDTH + (h + 1) * HEAD_DIM]
                o_ref[sl, col0 + h * HEAD_DIM:col0 + (h + 1) * HEAD_DIM] = _head_norm_gate(
                    o2[:, i * HEAD_DIM:(i + 1) * HEAD_DIM], gain_ref[...], gate)


RT_CHUNK = 256


def _tables_kernel(cos_ref, sin_ref, tril_ref, level_ref, dmat_ref, qdec_ref, kdec_ref):
    shape = cos_ref.shape
    half = HEAD_DIM // 2
    pos = lax.broadcasted_iota(jnp.int32, shape, 0).astype(F32)
    lane = lax.broadcasted_iota(jnp.int32, shape, 1)
    inv_freq = jnp.exp((lane % half).astype(F32) * (-math.log(ROPE_THETA) / half))
    ang = pos * inv_freq
    cos_ref[...] = jnp.cos(ang)
    sin = jnp.sin(ang)
    sin_ref[...] = jnp.where(lane < half, -sin, sin)
    tril_ref[...] = _chunk_tril_bf16(tril_ref.shape[0])
    level_ref[...] = _hg_level_matrix()
    n = RT_CHUNK
    row = lax.broadcasted_iota(jnp.int32, (n, n), 0)
    col = lax.broadcasted_iota(jnp.int32, (n, n), 1)
    dist = (row - col).astype(F32)
    p = lax.broadcasted_iota(jnp.int32, (n, HEAD_DIM), 0).astype(F32)
    for h in range(RT_HEADS):
        log_gamma = math.log(1.0 - 2.0 ** (-5.0 - h))
        dmat_ref[h] = jnp.exp(jnp.where(col <= row, dist * log_gamma, -jnp.inf))
        qdec_ref[h] = jnp.exp((p + 1.0) * log_gamma)
        kdec_ref[h] = jnp.exp((n - 1.0 - p) * log_gamma)


def _tables(seq, blk):
    f32 = lambda *shape: jax.ShapeDtypeStruct(shape, F32)
    out_shape = (f32(seq, HEAD_DIM), f32(seq, HEAD_DIM),
                 jax.ShapeDtypeStruct((blk, blk), BF16),
                 jax.ShapeDtypeStruct(((len(HG_LEVELS) + 1) * CHUNK, 2 * CHUNK), BF16),
                 f32(RT_HEADS, RT_CHUNK, RT_CHUNK), f32(RT_HEADS, RT_CHUNK, HEAD_DIM),
                 f32(RT_HEADS, RT_CHUNK, HEAD_DIM))
    return pl.pallas_call(_tables_kernel, out_shape=out_shape, name="tables")()


def _rt_body(x_ref, cos_ref, sin_ref, dmat_ref, qdec_ref, kdec_ref, gain_ref, o_ref, state_ref, *, blk, col0):
    n = RT_CHUNK

    for c in range(blk // n):
        sl = slice(c * n, (c + 1) * n)
        cos = cos_ref[sl, :]
        sin = sin_ref[sl, :]

        def rotary(x):
            return x * cos + pltpu.roll(x, HEAD_DIM // 2, axis=1) * sin

        for h in range(RT_HEADS):
            log_gamma = math.log(1.0 - 2.0 ** (-5.0 - h))
            cols = slice(h * HEAD_DIM, (h + 1) * HEAD_DIM)
            q = rotary(x_ref[sl, cols]) * (HEAD_DIM ** -0.5)
            k = rotary(x_ref[sl, RT_WIDTH + h * HEAD_DIM:RT_WIDTH + (h + 1) * HEAD_DIM])
            vb = _bf(x_ref[sl, 2 * RT_WIDTH + h * HEAD_DIM:2 * RT_WIDTH + (h + 1) * HEAD_DIM])
            att = _dot_nt(_bf(q), _bf(k)) * dmat_ref[h]
            state = state_ref[h]
            o = _dot(_bf(att), vb) + _dot(_bf(q * qdec_ref[h]), _bf(state))
            k_dec = k * kdec_ref[h]
            state_ref[h] = state * math.exp(n * log_gamma) + _dot_tn(_bf(k_dec), vb)
            gate = x_ref[sl, 3 * RT_WIDTH + h * HEAD_DIM:3 * RT_WIDTH + (h + 1) * HEAD_DIM]
            o_ref[sl, col0 + h * HEAD_DIM:col0 + (h + 1) * HEAD_DIM] = _head_norm_gate(o, gain_ref[...], gate)
            yield


def _mixer_kernel(qkv_ref, hg_ref, rt_ref, gate_ref, small_ref, alog_ref, dtb_ref, dn_gain_ref, lbt_ref,
                  hg_gain_ref, cos_ref, sin_ref, rt_gain_ref, tril_ref, level_ref, dmat_ref, qdec_ref, kdec_ref,
                  o_ref, dn_state, hg_state, rt_state, *, blk, layer, group):
    @pl.when(pl.program_id(1) == 0)
    def _():
        dn_state[...] = jnp.zeros_like(dn_state)
        hg_state[...] = jnp.zeros_like(hg_state)
        rt_state[...] = jnp.zeros_like(rt_state)

    live = [
        _dn_body(qkv_ref, gate_ref, small_ref, alog_ref, dtb_ref, dn_gain_ref, tril_ref, o_ref, dn_state,
                 blk=blk, group=group, col0=0),
        _hg_body(hg_ref, lbt_ref, hg_gain_ref, level_ref, o_ref, hg_state, blk=blk, layer=layer, col0=DN_WIDTH),
        _rt_body(rt_ref, cos_ref, sin_ref, dmat_ref, qdec_ref, kdec_ref, rt_gain_ref, o_ref, rt_state,
                 blk=blk, col0=DN_WIDTH + HG_WIDTH),
    ]
    while live:
        for body in list(live):
            if next(body, "done") == "done":
                live.remove(body)


MIXER_BLOCK = 512


def _mixer(qkv, rest, alog_row, dtb_row, dn_gain, lb_table, hg_gain, tables, rt_gain,
           *, layer, batch, seq, blk, group=4):
    cos_tab, sin_tab, tril, level_mat, rt_dmat, rt_qdec, rt_kdec = tables
    nblk = seq // blk
    t = batch * seq
    row = lambda b, s: (b * nblk + s)
    head_row = pl.BlockSpec((None, 1, HEAD_DIM), lambda b, s: (layer, 0, 0))
    lane_row = pl.BlockSpec((None, 1, LANES), lambda b, s: (layer, 0, 0))
    state = pltpu.VMEM((DN_HEADS, HEAD_DIM, HEAD_DIM), F32)
    return pl.pallas_call(
        functools.partial(_mixer_kernel, blk=blk, layer=layer, group=group),
        grid=(batch, nblk),
        in_specs=[
            pl.BlockSpec((blk, QKV_WIDTH), lambda b, s: (row(b, s), 0)),
            pl.BlockSpec((blk, 4 * HG_WIDTH), lambda b, s: (row(b, s), 0)),
            pl.BlockSpec((blk, 4 * RT_WIDTH), lambda b, s: (row(b, s), 1)),
            pl.BlockSpec((blk, DN_WIDTH), lambda b, s: (row(b, s), REST_GATE_BLOCK)),
            pl.BlockSpec((blk, LANES), lambda b, s: (row(b, s), REST_SMALL_BLOCK)),
            lane_row, lane_row, head_row,
            pl.BlockSpec((DEPTH, HG_WIDTH), lambda b, s: (0, 0)),
            head_row,
            pl.BlockSpec((blk, HEAD_DIM), lambda b, s: (s, 0)),
            pl.BlockSpec((blk, HEAD_DIM), lambda b, s: (s, 0)),
            head_row,
            pl.BlockSpec(tril.shape, lambda b, s: (0, 0)),
            pl.BlockSpec(level_mat.shape, lambda b, s: (0, 0)),
            pl.BlockSpec(rt_dmat.shape, lambda b, s: (0, 0, 0)),
            pl.BlockSpec(rt_qdec.shape, lambda b, s: (0, 0, 0)),
            pl.BlockSpec(rt_kdec.shape, lambda b, s: (0, 0, 0)),
        ],
        out_specs=pl.BlockSpec((blk, D_MODEL), lambda b, s: (row(b, s), 0)),
        out_shape=jax.ShapeDtypeStruct((t, D_MODEL), BF16),
        scratch_shapes=[state, pltpu.VMEM((HG_HEADS, HEAD_DIM, HEAD_DIM), F32),
                        pltpu.VMEM((RT_HEADS, HEAD_DIM, HEAD_DIM), F32)],
        compiler_params=_params("parallel", "arbitrary"),
        name="mixer",
    )(qkv, rest, rest, rest, rest, alog_row, dtb_row, dn_gain, lb_table, hg_gain, cos_tab, sin_tab, rt_gain,
      tril, level_mat, rt_dmat, rt_qdec, rt_kdec)


def _outproj_kernel(h_ref, x_ref, w_ref, o_ref):
    o_ref[...] = h_ref[...] + _dot(x_ref[...], w_ref[...])


def _outproj(h, mixed, w, *, layer, tm=512):
    t, d = h.shape
    tm = min(tm, t)
    return pl.pallas_call(
        _outproj_kernel,
        grid=(t // tm,),
        in_specs=[
            pl.BlockSpec((tm, d), lambda i: (i, 0)),
            pl.BlockSpec((tm, d), lambda i: (i, 0)),
            pl.BlockSpec((None, d, d), lambda i: (layer, 0, 0)),
        ],
        out_specs=pl.BlockSpec((tm, d), lambda i: (i, 0)),
        out_shape=jax.ShapeDtypeStruct((t, d), F32),
        compiler_params=_params("parallel"),
        name="outproj",
    )(h, mixed, w)


def _ple_kernel(h_ref, nw_ref, wg_ref, p_ref, wp_ref, fn_ref, o_ref, *, final):
    x = h_ref[...]
    gate = _sigmoid(_dot(_bf(_rms_norm(x, nw_ref[...])), wg_ref[...]))
    y = x + gate * _dot(_bf(p_ref[...]), wp_ref[...])
    if final:
        y = _rms_norm(y, fn_ref[...])
    o_ref[...] = y


def _ple(h, nw, wg, p, wp, fn, *, layer, final, tm=512):
    t, d = h.shape
    tm = min(tm, t)
    return pl.pallas_call(
        functools.partial(_ple_kernel, final=final),
        grid=(t // tm,),
        in_specs=[
            pl.BlockSpec((tm, d), lambda i: (i, 0)),
            pl.BlockSpec((None, 1, d), lambda i: (layer, 0, 0)),
            pl.BlockSpec((None, d, d), lambda i: (layer, 0, 0)),
            pl.BlockSpec((None, tm, PLE_DIM), lambda i: (layer, i, 0)),
            pl.BlockSpec((None, PLE_DIM, d), lambda i: (layer, 0, 0)),
            pl.BlockSpec((1, d), lambda i: (0, 0)),
        ],
        out_specs=pl.BlockSpec((tm, d), lambda i: (i, 0)),
        out_shape=jax.ShapeDtypeStruct((t, d), F32),
        compiler_params=_params("parallel"),
        name="ple",
    )(h, nw, wg, p, wp, fn)


PREP_COLS = 256


def _w_qkv_kernel(w_ref, o_ref):
    o_ref[...] = _bf(w_ref[...])


def _w_rest_kernel(a_ref, b_ref, o_ref, *, n_shift, n_gate):
    j = pl.program_id(1)
    off = 2 * DN_HEADS
    lane = lax.broadcasted_iota(jnp.int32, o_ref.shape, 1)

    @pl.when(j < n_shift)
    def _():
        a = pltpu.roll(a_ref[...], PREP_COLS - off, axis=1)
        b = pltpu.roll(b_ref[...], LANES - off, axis=1)
        tail = _cat1(*([b] * (PREP_COLS // LANES)))
        o_ref[...] = _bf(jnp.where(lane < PREP_COLS - off, a, tail))

    @pl.when((j >= n_shift) & (j < n_shift + n_gate))
    def _():
        o_ref[...] = _bf(a_ref[...])

    @pl.when(j == n_shift + n_gate)
    def _():
        o_ref[...] = _bf(jnp.where(lane < off, a_ref[...], 0.0))


def _prep_w_in(w):
    depth, d, _ = w.shape
    w_qkv = pl.pallas_call(
        _w_qkv_kernel,
        grid=(depth, QKV_WIDTH // PREP_COLS),
        in_specs=[pl.BlockSpec((None, d, PREP_COLS), lambda l, j: (l, 0, j))],
        out_specs=pl.BlockSpec((None, d, PREP_COLS), lambda l, j: (l, 0, j)),
        out_shape=jax.ShapeDtypeStruct((depth, d, QKV_WIDTH), BF16),
        compiler_params=_params("parallel", "parallel"),
        name="w_qkv_prep",
    )(w)
    n_shift = (4 * HG_WIDTH + 4 * RT_WIDTH) // PREP_COLS
    n_gate = DN_WIDTH // PREP_COLS
    gate0 = QKV_WIDTH // PREP_COLS
    small0 = (QKV_WIDTH + DN_WIDTH) // PREP_COLS
    per = PREP_COLS // LANES

    def a_block(l, j):
        return l, 0, jnp.where(j < n_shift, small0 + j, jnp.where(j < n_shift + n_gate, gate0 + j - n_shift, small0))

    def b_block(l, j):
        return l, 0, jnp.where(j < n_shift, per * (small0 + j + 1), 0)

    w_rest = pl.pallas_call(
        functools.partial(_w_rest_kernel, n_shift=n_shift, n_gate=n_gate),
        grid=(depth, REST_WIDTH // PREP_COLS),
        in_specs=[pl.BlockSpec((None, d, PREP_COLS), a_block), pl.BlockSpec((None, d, LANES), b_block)],
        out_specs=pl.BlockSpec((None, d, PREP_COLS), lambda l, j: (l, 0, j)),
        out_shape=jax.ShapeDtypeStruct((depth, d, REST_WIDTH), BF16),
        compiler_params=_params("parallel", "parallel"),
        name="w_rest_prep",
    )(w, w)
    return w_qkv, w_rest


def _lane_rows(x, offset):
    n = x.shape[1]
    return jnp.pad(x.astype(F32), ((0, 0), (offset, LANES - offset - n)))[:, None, :]


def kernel(x, p, ffn1_norm, ffn1_w_gate_up, ffn1_w_down, mix_norm, w_in, dn_conv, dn_a_log, dn_dt_bias, dn_out_norm, hg_lower_bounds, hg_out_norm, rt_out_norm, w_out, ffn2_norm, ffn2_w_gate_up, ffn2_w_down, ple_norm, ple_w_gate, ple_w_proj, final_norm):
    batch, seq, d = x.shape
    t = batch * seq
    h = x.reshape(t, d)
    blk = min(MIXER_BLOCK, seq)
    tables = _tables(seq, blk)
    rows = lambda v: v.astype(F32)[:, None, :]
    ffn1_wgu, ffn1_wd = _bf(ffn1_w_gate_up), _bf(ffn1_w_down)
    ffn2_wgu, ffn2_wd = _bf(ffn2_w_gate_up), _bf(ffn2_w_down)
    w_qkv, w_rest = _prep_w_in(w_in)
    w_out_bf, ple_wg, ple_wp = _bf(w_out), _bf(ple_w_gate), _bf(ple_w_proj)
    ffn1_nw, ffn2_nw, mix_nw, ple_nw = rows(ffn1_norm), rows(ffn2_norm), rows(mix_norm), rows(ple_norm)
    dn_gain, hg_gain, rt_gain = rows(dn_out_norm), rows(hg_out_norm), rows(rt_out_norm)
    alog_rows, dtb_rows = _lane_rows(dn_a_log, DN_HEADS), _lane_rows(dn_dt_bias, DN_HEADS)
    p_tok = p.reshape(DEPTH, t, PLE_DIM)
    final_nw = final_norm.astype(F32)[None, :]
    for i in range(DEPTH):
        h = _ffn(h, ffn1_nw, ffn1_wgu, ffn1_wd, layer=i)
        qkv = _qkvproj(h, mix_nw, w_qkv, dn_conv, layer=i, seq=seq)
        rest = _restproj(h, mix_nw, w_rest, layer=i)
        mixed = _mixer(qkv, rest, alog_rows, dtb_rows, dn_gain, hg_lower_bounds, hg_gain, tables, rt_gain,
                       layer=i, batch=batch, seq=seq, blk=blk)
        h = _outproj(h, mixed, w_out_bf, layer=i)
        h = _ffn(h, ffn2_nw, ffn2_wgu, ffn2_wd, layer=i)
        h = _ple(h, ple_nw, ple_wg, p_tok, ple_wp, final_nw, layer=i, final=(i == DEPTH - 1))
    return h.reshape(batch, seq, d)
```

```python
import functools
import math

import jax
import jax.numpy as jnp
from jax import lax
from jax.experimental import pallas as pl
from jax.experimental.pallas import tpu as pltpu

F32 = jnp.float32
BF16 = jnp.bfloat16

D_MODEL = 2048
D_FF = 5632
DEPTH = 2
DN_HEADS = 8
HG_HEADS = 4
RT_HEADS = 4
HEAD_DIM = 128
DN_WIDTH = DN_HEADS * HEAD_DIM
HG_WIDTH = HG_HEADS * HEAD_DIM
RT_WIDTH = RT_HEADS * HEAD_DIM
QKV_WIDTH = 3 * DN_WIDTH
CONV_WIDTH = 4
PLE_DIM = 256
ROPE_THETA = 10000.0
NORM_EPS = 1e-6
LOG2_E = math.log2(math.e)

LANES = 128
SUBLANES = 8
VMEM_LIMIT = 58 * 1024 * 1024

REST_MAIN = 4 * HG_WIDTH + 4 * RT_WIDTH + DN_WIDTH
REST_WIDTH = REST_MAIN + 256
REST_GATE_BLOCK = (4 * HG_WIDTH + 4 * RT_WIDTH) // DN_WIDTH
REST_SMALL_BLOCK = REST_MAIN // LANES

CHUNK = 128
HG_LEVELS = (64, 32, 16, 8, 4, 2, 1)


def _dot(a, b):
    return jnp.dot(a, b, preferred_element_type=F32)


def _dot_nt(a, b):
    return lax.dot_general(a, b, (((1,), (1,)), ((), ())), preferred_element_type=F32)


def _dot_tn(a, b):
    return lax.dot_general(a, b, (((0,), (0,)), ((), ())), preferred_element_type=F32)


def _bf(x):
    return x.astype(BF16)


def _sigmoid(x):
    return 1.0 / (1.0 + jnp.exp(-x))


def _silu(x):
    return x * _sigmoid(x)


def _rms_norm(x, w):
    return x * lax.rsqrt(jnp.mean(x * x, axis=-1, keepdims=True) + NORM_EPS) * w


def _neg_abs(x):
    bits = lax.bitcast_convert_type(x, jnp.uint32) | jnp.uint32(0x80000000)
    return lax.bitcast_convert_type(bits, F32)


def _split_bf16(x):
    hi = x.astype(BF16)
    lo = (x - hi.astype(F32)).astype(BF16)
    return hi, lo


def _params(*sem):
    return pltpu.CompilerParams(dimension_semantics=sem, vmem_limit_bytes=VMEM_LIMIT)


def _ffn_kernel(h_ref, nw_ref, wg_ref, wu_ref, wd_ref, o_ref, hn_ref, *, nf):
    j = pl.program_id(1)

    def partial_sum():
        hn = hn_ref[...]
        g = _dot(hn, wg_ref[...])
        u = _dot(hn, wu_ref[...])
        return _dot(_bf(_silu(g) * u), wd_ref[...])

    @pl.when(j == 0)
    def _():
        hn_ref[...] = _bf(_rms_norm(h_ref[...], nw_ref[...]))
        o_ref[...] = partial_sum()

    @pl.when((j > 0) & (j < nf - 1))
    def _():
        o_ref[...] += partial_sum()

    @pl.when(j == nf - 1)
    def _():
        o_ref[...] = h_ref[...] + 0.5 * (o_ref[...] + partial_sum())


def _ffn(h, nw, wgu, wd, *, layer, tm=1024, tf=512):
    t, d = h.shape
    tm = min(tm, t)
    nf = D_FF // tf
    return pl.pallas_call(
        functools.partial(_ffn_kernel, nf=nf),
        grid=(t // tm, nf),
        in_specs=[
            pl.BlockSpec((tm, d), lambda i, j: (i, 0)),
            pl.BlockSpec((None, 1, d), lambda i, j: (layer, 0, 0)),
            pl.BlockSpec((None, d, tf), lambda i, j: (layer, 0, j)),
            pl.BlockSpec((None, d, tf), lambda i, j: (layer, 0, j + nf)),
            pl.BlockSpec((None, tf, d), lambda i, j: (layer, j, 0)),
        ],
        out_specs=pl.BlockSpec((tm, d), lambda i, j: (i, 0)),
        out_shape=jax.ShapeDtypeStruct((t, d), F32),
        scratch_shapes=[pltpu.VMEM((tm, d), BF16)],
        compiler_params=_params("parallel", "arbitrary"),
        name="ffn",
    )(h, nw, wgu, wgu, wd)


def _qkv_kernel(h_ref, nw_ref, w_ref, cw_ref, o_ref, hn_ref, tail_ref, *, tiles_per_seq, bounds):
    i = pl.program_id(0)
    first = (i % tiles_per_seq) == 0
    tm = h_ref.shape[0]
    hn_ref[...] = _bf(_rms_norm(h_ref[...], nw_ref[...]))
    chunks = [slice(a, b) for a, b in zip(bounds[:-1], bounds[1:])]
    y_next = _dot(hn_ref[...], w_ref[:, chunks[0]])
    for c, cs in enumerate(chunks):
        y = y_next
        if c + 1 < len(chunks):
            y_next = _dot(hn_ref[...], w_ref[:, chunks[c + 1]])
        tail = jnp.where(first, 0.0, tail_ref[:, cs])
        tail_ref[:, cs] = y[tm - SUBLANES:, :]
        z = jnp.concatenate([tail, y], axis=0)
        acc = y * cw_ref[CONV_WIDTH - 1:CONV_WIDTH, cs]
        for j in range(CONV_WIDTH - 1):
            shifted = pltpu.roll(z, CONV_WIDTH - 1 - j, axis=0)[SUBLANES:, :]
            acc = acc + shifted * cw_ref[j:j + 1, cs]
        x = _silu(acc)
        for hh in range((cs.stop - cs.start) // HEAD_DIM):
            col = cs.start + hh * HEAD_DIM
            xh = x[:, hh * HEAD_DIM:(hh + 1) * HEAD_DIM]
            if col < 2 * DN_WIDTH:
                r = lax.rsqrt(jnp.sum(xh * xh, axis=-1, keepdims=True) + NORM_EPS)
                if col < DN_WIDTH:
                    r = r * (HEAD_DIM ** -0.5)
                xh = xh * r
            o_ref[:, col:col + HEAD_DIM] = xh
        zero = jnp.minimum(jnp.abs(xh[0:2 * SUBLANES, :]), 0.0)
        hn_ref[0:2 * SUBLANES, 0:HEAD_DIM] += _bf(zero)


QKV_CHUNK_BOUNDS = (0, 256, 768, 1280, 1792, 2304, 2816, 3072)


def _qkvproj(h, nw, w, conv_w, *, layer, seq, tm=512, bounds=QKV_CHUNK_BOUNDS):
    t, d = h.shape
    tm = min(tm, seq)
    return pl.pallas_call(
        functools.partial(_qkv_kernel, tiles_per_seq=seq // tm, bounds=bounds),
        grid=(t // tm,),
        in_specs=[
            pl.BlockSpec((tm, d), lambda i: (i, 0)),
            pl.BlockSpec((None, 1, d), lambda i: (layer, 0, 0)),
            pl.BlockSpec((None, d, QKV_WIDTH), lambda i: (layer, 0, 0)),
            pl.BlockSpec((None, CONV_WIDTH, QKV_WIDTH), lambda i: (layer, 0, 0)),
        ],
        out_specs=pl.BlockSpec((tm, QKV_WIDTH), lambda i: (i, 0)),
        out_shape=jax.ShapeDtypeStruct((t, QKV_WIDTH), F32),
        scratch_shapes=[pltpu.VMEM((tm, d), BF16), pltpu.VMEM((SUBLANES, QKV_WIDTH), F32)],
        compiler_params=_params("arbitrary"),
        name="qkvproj",
    )(h, nw, w, conv_w)


def _restproj_kernel(h_ref, nw_ref, w_ref, o_ref, hn_ref):
    @pl.when(pl.program_id(1) == 0)
    def _():
        hn_ref[...] = _bf(_rms_norm(h_ref[...], nw_ref[...]))

    o_ref[...] = _dot(hn_ref[...], w_ref[...])


def _restproj(h, nw, w, *, layer, tm=1024, tn=1792):
    t, d = h.shape
    tm = min(tm, t)
    n = w.shape[-1]
    return pl.pallas_call(
        _restproj_kernel,
        grid=(t // tm, n // tn),
        in_specs=[
            pl.BlockSpec((tm, d), lambda i, j: (i, 0)),
            pl.BlockSpec((None, 1, d), lambda i, j: (layer, 0, 0)),
            pl.BlockSpec((None, d, tn), lambda i, j: (layer, 0, j)),
        ],
        out_specs=pl.BlockSpec((tm, tn), lambda i, j: (i, j)),
        out_shape=jax.ShapeDtypeStruct((t, n), F32),
        scratch_shapes=[pltpu.VMEM((tm, d), BF16)],
        compiler_params=_params("parallel", "arbitrary"),
        name="restproj",
    )(h, nw, w)


def _chunk_tril_bf16(blk):
    r = lax.broadcasted_iota(jnp.int32, (blk, blk), 0)
    c = lax.broadcasted_iota(jnp.int32, (blk, blk), 1)
    return jnp.where((r // CHUNK == c // CHUNK) & (c <= r), 1.0, 0.0).astype(BF16)


def _cat1(*xs):
    return jnp.concatenate(xs, axis=1)


def _bd(a, b):
    z = jnp.zeros_like(a)
    return jnp.concatenate([_cat1(a, z), _cat1(z, b)], axis=0)


def _head_norm_gate(o, gain, gate):
    o = o * lax.rsqrt(jnp.mean(o * o, axis=-1, keepdims=True) + NORM_EPS) * gain
    return _bf(o * _silu(gate))


def _dn_body(qkv_ref, gate_ref, small_ref, alog_ref, dtb_ref, gain_ref, tril_ref, o_ref, state_ref,
             *, blk, group, col0):
    nc = blk // CHUNK
    half = CHUNK // 2

    small = small_ref[...]
    beta_all = _sigmoid(small)
    x = small + dtb_ref[...]
    softplus = jnp.maximum(x, 0.0) + jnp.log(1.0 + jnp.exp(-jnp.abs(x)))
    g_all = -(jnp.exp(alog_ref[...]) * softplus) * LOG2_E
    tril = tril_ref[...]
    g_hi, g_lo = _split_bf16(g_all)
    gc_all = _dot(tril, g_hi) + _dot(tril, g_lo)
    gc_t = gc_all.T

    row = lax.broadcasted_iota(jnp.int32, (CHUNK, CHUNK), 0)
    col = lax.broadcasted_iota(jnp.int32, (CHUNK, CHUNK), 1)
    incl = col <= row
    same_half = (row // half) == (col // half)
    diag_blocks = same_half & (col < row)
    off_block = (row >= half) & (col < half)

    for g0 in range(0, DN_HEADS, group):
        heads = list(range(g0, g0 + group))
        pairs = [(heads[i], heads[i + 1]) for i in range(0, group, 2)]

        hd = {}
        for h in heads:
            q = qkv_ref[:, h * HEAD_DIM:(h + 1) * HEAD_DIM]
            k = qkv_ref[:, DN_WIDTH + h * HEAD_DIM:DN_WIDTH + (h + 1) * HEAD_DIM]
            v = qkv_ref[:, 2 * DN_WIDTH + h * HEAD_DIM:2 * DN_WIDTH + (h + 1) * HEAD_DIM]
            beta = jnp.broadcast_to(beta_all[:, h:h + 1], (blk, HEAD_DIM))
            gc = jnp.broadcast_to(gc_all[:, DN_HEADS + h:DN_HEADS + h + 1], (blk, HEAD_DIM))
            eg = jnp.exp2(gc)
            kbn = -(k * beta)
            hd[h] = dict(k=k, gc=gc, kb=_bf(k), lhs=(_bf(kbn), _bf(q)),
                         rhs=_bf(jnp.concatenate([v * beta, kbn * eg], axis=1)),
                         qeg=_bf(q * eg), gc_row=gc_t[DN_HEADS + h:DN_HEADS + h + 1, :])
        yield

        ln_bd, ln_off, qk, k_dec = {}, {}, {}, {}
        for c in range(nc):
            sl = slice(c * CHUNK, (c + 1) * CHUNK)
            for pr in pairs:
                lhs = _cat1(*[jnp.concatenate([hd[h]["lhs"][0][sl], hd[h]["lhs"][1][sl]], axis=0) for h in pr])
                a2 = _dot_nt(lhs, _bd(hd[pr[0]]["kb"][sl], hd[pr[1]]["kb"][sl]))
                for i, h in enumerate(pr):
                    d = hd[h]
                    a = a2[:, i * CHUNK:(i + 1) * CHUNK]
                    gcc = d["gc"][sl]
                    decay = jnp.exp2(jnp.where(incl, gcc - d["gc_row"][:, sl], -jnp.inf))
                    ln = a[:CHUNK] * decay
                    ln_bd[h, c] = jnp.where(diag_blocks, ln, 0.0)
                    ln_off[h, c] = jnp.where(off_block, ln, 0.0)
                    qk[h, c] = _bf(a[CHUNK:] * decay)
                    g_last = gcc[CHUNK - 1:CHUNK, :]
                    k_dec[h, c] = _bf(d["k"][sl] * jnp.exp2(g_last - gcc))
            yield

        units = [(pr, c) for c in range(nc) for pr in pairs]

        def pair_dot(xs, ws):
            y = _dot(_cat1(*xs), _bd(*ws))
            n = ws[0].shape[1]
            return y[:, :n], y[:, n:]

        p = dict(ln_bd)
        m = {}
        for (pr, c) in units:
            bs = [_bf(ln_bd[h, c]) for h in pr]
            for h, y in zip(pr, pair_dot(bs, bs)):
                m[h, c] = y
        yield
        for _ in range(int(math.log2(half)) - 2):
            for (pr, c) in units:
                mbs = [_bf(m[h, c]) for h in pr]
                xs = [jnp.concatenate([mb, _bf(p[h, c])], axis=0) for mb, h in zip(mbs, pr)]
                for h, x2 in zip(pr, pair_dot(xs, mbs)):
                    p[h, c] = p[h, c] + m[h, c] + x2[CHUNK:]
                    m[h, c] = x2[:CHUNK]
            yield
        for (pr, c) in units:
            ys = pair_dot([_bf(p[h, c]) for h in pr], [_bf(m[h, c]) for h in pr])
            for h, y in zip(pr, ys):
                p[h, c] = p[h, c] + m[h, c] + y
        yield
        for (pr, c) in units:
            pbs = [_bf(p[h, c]) for h in pr]
            ys = pair_dot([_bf(ln_off[h, c]) for h in pr], pbs)
            ys = [ln_off[h, c] + y for h, y in zip(pr, ys)]
            zs = pair_dot(pbs, [_bf(y) for y in ys])
            for h, y, z in zip(pr, ys, zs):
                p[h, c] = p[h, c] + y + z
        yield
        sol = {}
        for c in range(nc):
            for h in heads:
                rhs = hd[h]["rhs"][c * CHUNK:(c + 1) * CHUNK]
                sol[h, c] = rhs.astype(F32) + _dot(_bf(p[h, c]), rhs)
        yield

        states = {h: state_ref[h] for h in heads}
        outs = {h: [] for h in heads}
        for c in range(nc):
            sl = slice(c * CHUNK, (c + 1) * CHUNK)
            for pr in pairs:
                xs = [jnp.concatenate([_bf(sol[h, c][:, HEAD_DIM:]), hd[h]["qeg"][sl]], axis=0) for h in pr]
                wss = pair_dot(xs, [_bf(states[h]) for h in pr])
                v_new = [_bf(sol[h, c][:, :HEAD_DIM] + ws[:CHUNK]) for h, ws in zip(pr, wss)]
                intra = pair_dot([qk[h, c] for h in pr], v_new)
                for h, ws, vn, it in zip(pr, wss, v_new, intra):
                    outs[h].append(ws[CHUNK:] + it)
                    g_last = hd[h]["gc"][(c + 1) * CHUNK - 1:(c + 1) * CHUNK, :]
                    states[h] = states[h] * jnp.exp2(g_last) + _dot_tn(k_dec[h, c], vn)
            yield
        for h in heads:
            state_ref[h] = states[h]
            o = jnp.concatenate(outs[h], axis=0) if nc > 1 else outs[h][0]
            cols = slice(h * HEAD_DIM, (h + 1) * HEAD_DIM)
            o_ref[:, col0 + h * HEAD_DIM:col0 + (h + 1) * HEAD_DIM] = _head_norm_gate(
                o, gain_ref[...], gate_ref[:, cols])


def _hg_level_matrix():
    r = lax.broadcasted_iota(jnp.int32, (CHUNK, CHUNK), 0)
    c = lax.broadcasted_iota(jnp.int32, (CHUNK, CHUNK), 1)
    mats = [c <= r]
    for m in HG_LEVELS:
        mats.append(c <= (r // (2 * m)) * (2 * m) + m - 1)
    mat = jnp.concatenate([jnp.where(x, 1.0, 0.0).astype(BF16) for x in mats], axis=0)
    return _cat1(mat, mat)


def _hg_body(x_ref, lbt_ref, gain_ref, level_ref, o_ref, state_ref, *, blk, layer, col0):
    tbl = lbt_ref[...]
    e = jnp.exp(tbl - jnp.max(tbl, axis=0, keepdims=True))
    sm = e / jnp.sum(e, axis=0, keepdims=True)
    lb = jnp.zeros((1, HG_WIDTH), F32)
    for l in range(1, layer + 1):
        lb = lb + sm[l:l + 1, :]

    level_mat = level_ref[...]
    row = lax.broadcasted_iota(jnp.int32, (CHUNK, CHUNK), 0)
    col = lax.broadcasted_iota(jnp.int32, (CHUNK, CHUNK), 1)
    rowv = lax.broadcasted_iota(jnp.int32, (CHUNK, HEAD_DIM), 0)
    eye = row == col
    pair_mask = [((row // (2 * m)) == (col // (2 * m))) & ((row // m) % 2 == 1) & ((col // m) % 2 == 0)
                 for m in HG_LEVELS]
    lower_rows = [(rowv // m) % 2 == 1 for m in HG_LEVELS]
    pairs = [(h, h + 1) for h in range(0, HG_HEADS, 2)]

    for c in range(blk // CHUNK):
        sl = slice(c * CHUNK, (c + 1) * CHUNK)
        z = x_ref[sl, HG_WIDTH:2 * HG_WIDTH]
        sig = _sigmoid(z)
        log2_f = jnp.log(lb + (1.0 - lb) * sig) * LOG2_E
        hk_all = (1.0 - lb) * (1.0 - sig)
        lf_hi, lf_lo = _split_bf16(log2_f)
        cums = _dot(level_mat, jnp.concatenate([lf_hi, lf_lo], axis=0))
        yield
        for pr in pairs:
            q, k, v, b, state_t = {}, {}, {}, {}, {}
            for h in pr:
                cols = slice(h * HEAD_DIM, (h + 1) * HEAD_DIM)
                q[h] = _silu(x_ref[sl, cols]) * (HEAD_DIM ** -0.5)
                k[h] = hk_all[:, cols]
                v[h] = _bf(x_ref[sl, 2 * HG_WIDTH + h * HEAD_DIM:2 * HG_WIDTH + (h + 1) * HEAD_DIM])
                b[h] = cums[0:CHUNK, cols]
                state_t[h] = state_ref[h]
            scores = [_dot_nt(_cat1(*[_bf(q[h]) for h in pr]), _bd(*[_bf(k[h]) for h in pr]))]
            for li in range(len(HG_LEVELS)):
                xs = []
                for h in pr:
                    ref = cums[(li + 1) * CHUNK:(li + 2) * CHUNK, h * HEAD_DIM:(h + 1) * HEAD_DIM]
                    scale = jnp.exp2(_neg_abs(b[h] - ref))
                    xs.append(_bf(jnp.where(lower_rows[li], q[h], k[h]) * scale))
                scores.append(_dot_nt(_cat1(*xs), _bd(*xs)))
                if li % 2 == 1:
                    yield
            att = []
            for i, h in enumerate(pr):
                a = jnp.zeros((CHUNK, CHUNK), F32)
                for li in range(len(HG_LEVELS)):
                    a = jnp.where(pair_mask[li], scores[li + 1][:, i * CHUNK:(i + 1) * CHUNK], a)
                att.append(_bf(jnp.where(eye, scores[0][:, i * CHUNK:(i + 1) * CHUNK], a)))
            intra = _dot(_cat1(*att), _bd(*[v[h] for h in pr]))
            inter = _dot_nt(_cat1(*[_bf(q[h] * jnp.exp2(b[h])) for h in pr]),
                            _bd(*[_bf(state_t[h]) for h in pr]))
            o2 = intra + inter
            yield
            for i, h in enumerate(pr):
                b_last = b[h][CHUNK - 1:CHUNK, :]
                k_dec = k[h] * jnp.exp2(b_last - b[h])
                state_ref[h] = state_t[h] * jnp.exp2(b_last) + _dot_tn(v[h], _bf(k_dec))
                gate = x_ref[sl, 3 * HG_WIDTH + h * HEAD_DIM:3 * HG_WIDTH + (h + 1) * HEAD_DIM]
                o_ref[sl, col0 + h * HEAD_DIM:col0 + (h + 1) * HEAD_DIM] = _head_norm_gate(
                    o2[:, i * HEAD_DIM:(i + 1) * HEAD_DIM], gain_ref[...], gate)


RT_CHUNK = 256


def _tables_kernel(cos_ref, sin_ref, tril_ref, level_ref, dmat_ref, qdec_ref, kdec_ref):
    shape = cos_ref.shape
    half = HEAD_DIM // 2
    pos = lax.broadcasted_iota(jnp.int32, shape, 0).astype(F32)
    lane = lax.broadcasted_iota(jnp.int32, shape, 1)
    inv_freq = jnp.exp((lane % half).astype(F32) * (-math.log(ROPE_THETA) / half))
    ang = pos * inv_freq
    cos_ref[...] = jnp.cos(ang)
    sin = jnp.sin(ang)
    sin_ref[...] = jnp.where(lane < half, -sin, sin)
    tril_ref[...] = _chunk_tril_bf16(tril_ref.shape[0])
    level_ref[...] = _hg_level_matrix()
    n = RT_CHUNK
    row = lax.broadcasted_iota(jnp.int32, (n, n), 0)
    col = lax.broadcasted_iota(jnp.int32, (n, n), 1)
    dist = (row - col).astype(F32)
    p = lax.broadcasted_iota(jnp.int32, (n, HEAD_DIM), 0).astype(F32)
    for h in range(RT_HEADS):
        log_gamma = math.log(1.0 - 2.0 ** (-5.0 - h))
        dmat_ref[h] = jnp.exp(jnp.where(col <= row, dist * log_gamma, -jnp.inf))
        qdec_ref[h] = jnp.exp((p + 1.0) * log_gamma)
        kdec_ref[h] = jnp.exp((n - 1.0 - p) * log_gamma)


def _tables(seq, blk):
    f32 = lambda *shape: jax.ShapeDtypeStruct(shape, F32)
    out_shape = (f32(seq, HEAD_DIM), f32(seq, HEAD_DIM),
                 jax.ShapeDtypeStruct((blk, blk), BF16),
                 jax.ShapeDtypeStruct(((len(HG_LEVELS) + 1) * CHUNK, 2 * CHUNK), BF16),
                 f32(RT_HEADS, RT_CHUNK, RT_CHUNK), f32(RT_HEADS, RT_CHUNK, HEAD_DIM),
                 f32(RT_HEADS, RT_CHUNK, HEAD_DIM))
    return pl.pallas_call(_tables_kernel, out_shape=out_shape, name="tables")()


def _rt_body(x_ref, cos_ref, sin_ref, dmat_ref, qdec_ref, kdec_ref, gain_ref, o_ref, state_ref, *, blk, col0):
    n = RT_CHUNK

    for c in range(blk // n):
        sl = slice(c * n, (c + 1) * n)
        cos = cos_ref[sl, :]
        sin = sin_ref[sl, :]

        def rotary(x):
            return x * cos + pltpu.roll(x, HEAD_DIM // 2, axis=1) * sin

        for h in range(RT_HEADS):
            log_gamma = math.log(1.0 - 2.0 ** (-5.0 - h))
            cols = slice(h * HEAD_DIM, (h + 1) * HEAD_DIM)
            q = rotary(x_ref[sl, cols]) * (HEAD_DIM ** -0.5)
            k = rotary(x_ref[sl, RT_WIDTH + h * HEAD_DIM:RT_WIDTH + (h + 1) * HEAD_DIM])
            vb = _bf(x_ref[sl, 2 * RT_WIDTH + h * HEAD_DIM:2 * RT_WIDTH + (h + 1) * HEAD_DIM])
            att = _dot_nt(_bf(q), _bf(k)) * dmat_ref[h]
            state = state_ref[h]
            o = _dot(_bf(att), vb) + _dot(_bf(q * qdec_ref[h]), _bf(state))
            k_dec = k * kdec_ref[h]
            state_ref[h] = state * math.exp(n * log_gamma) + _dot_tn(_bf(k_dec), vb)
            gate = x_ref[sl, 3 * RT_WIDTH + h * HEAD_DIM:3 * RT_WIDTH + (h + 1) * HEAD_DIM]
            o_ref[sl, col0 + h * HEAD_DIM:col0 + (h + 1) * HEAD_DIM] = _head_norm_gate(o, gain_ref[...], gate)
            yield


def _mixer_kernel(qkv_ref, hg_ref, rt_ref, gate_ref, small_ref, alog_ref, dtb_ref, dn_gain_ref, lbt_ref,
                  hg_gain_ref, cos_ref, sin_ref, rt_gain_ref, tril_ref, level_ref, dmat_ref, qdec_ref, kdec_ref,
                  o_ref, dn_state, hg_state, rt_state, *, blk, layer, group):
    @pl.when(pl.program_id(1) == 0)
    def _():
        dn_state[...] = jnp.zeros_like(dn_state)
        hg_state[...] = jnp.zeros_like(hg_state)
        rt_state[...] = jnp.zeros_like(rt_state)

    live = [
        _dn_body(qkv_ref, gate_ref, small_ref, alog_ref, dtb_ref, dn_gain_ref, tril_ref, o_ref, dn_state,
                 blk=blk, group=group, col0=0),
        _hg_body(hg_ref, lbt_ref, hg_gain_ref, level_ref, o_ref, hg_state, blk=blk, layer=layer, col0=DN_WIDTH),
        _rt_body(rt_ref, cos_ref, sin_ref, dmat_ref, qdec_ref, kdec_ref, rt_gain_ref, o_ref, rt_state,
                 blk=blk, col0=DN_WIDTH + HG_WIDTH),
    ]
    while live:
        for body in list(live):
            if next(body, "done") == "done":
                live.remove(body)


MIXER_BLOCK = 512


def _mixer(qkv, rest, alog_row, dtb_row, dn_gain, lb_table, hg_gain, tables, rt_gain,
           *, layer, batch, seq, blk, group=4):
    cos_tab, sin_tab, tril, level_mat, rt_dmat, rt_qdec, rt_kdec = tables
    nblk = seq // blk
    t = batch * seq
    row = lambda b, s: (b * nblk + s)
    head_row = pl.BlockSpec((None, 1, HEAD_DIM), lambda b, s: (layer, 0, 0))
    lane_row = pl.BlockSpec((None, 1, LANES), lambda b, s: (layer, 0, 0))
    state = pltpu.VMEM((DN_HEADS, HEAD_DIM, HEAD_DIM), F32)
    return pl.pallas_call(
        functools.partial(_mixer_kernel, blk=blk, layer=layer, group=group),
        grid=(batch, nblk),
        in_specs=[
            pl.BlockSpec((blk, QKV_WIDTH), lambda b, s: (row(b, s), 0)),
            pl.BlockSpec((blk, 4 * HG_WIDTH), lambda b, s: (row(b, s), 0)),
            pl.BlockSpec((blk, 4 * RT_WIDTH), lambda b, s: (row(b, s), 1)),
            pl.BlockSpec((blk, DN_WIDTH), lambda b, s: (row(b, s), REST_GATE_BLOCK)),
            pl.BlockSpec((blk, LANES), lambda b, s: (row(b, s), REST_SMALL_BLOCK)),
            lane_row, lane_row, head_row,
            pl.BlockSpec((DEPTH, HG_WIDTH), lambda b, s: (0, 0)),
            head_row,
            pl.BlockSpec((blk, HEAD_DIM), lambda b, s: (s, 0)),
            pl.BlockSpec((blk, HEAD_DIM), lambda b, s: (s, 0)),
            head_row,
            pl.BlockSpec(tril.shape, lambda b, s: (0, 0)),
            pl.BlockSpec(level_mat.shape, lambda b, s: (0, 0)),
            pl.BlockSpec(rt_dmat.shape, lambda b, s: (0, 0, 0)),
            pl.BlockSpec(rt_qdec.shape, lambda b, s: (0, 0, 0)),
            pl.BlockSpec(rt_kdec.shape, lambda b, s: (0, 0, 0)),
        ],
        out_specs=pl.BlockSpec((blk, D_MODEL), lambda b, s: (row(b, s), 0)),
        out_shape=jax.ShapeDtypeStruct((t, D_MODEL), BF16),
        scratch_shapes=[state, pltpu.VMEM((HG_HEADS, HEAD_DIM, HEAD_DIM), F32),
                        pltpu.VMEM((RT_HEADS, HEAD_DIM, HEAD_DIM), F32)],
        compiler_params=_params("parallel", "arbitrary"),
        name="mixer",
    )(qkv, rest, rest, rest, rest, alog_row, dtb_row, dn_gain, lb_table, hg_gain, cos_tab, sin_tab, rt_gain,
      tril, level_mat, rt_dmat, rt_qdec, rt_kdec)


def _outproj_kernel(h_ref, x_ref, w_ref, o_ref):
    o_ref[...] = h_ref[...] + _dot(x_ref[...], w_ref[...])


def _outproj(h, mixed, w, *, layer, tm=512):
    t, d = h.shape
    tm = min(tm, t)
    return pl.pallas_call(
        _outproj_kernel,
        grid=(t // tm,),
        in_specs=[
            pl.BlockSpec((tm, d), lambda i: (i, 0)),
            pl.BlockSpec((tm, d), lambda i: (i, 0)),
            pl.BlockSpec((None, d, d), lambda i: (layer, 0, 0)),
        ],
        out_specs=pl.BlockSpec((tm, d), lambda i: (i, 0)),
        out_shape=jax.ShapeDtypeStruct((t, d), F32),
        compiler_params=_params("parallel"),
        name="outproj",
    )(h, mixed, w)


def _ple_kernel(h_ref, nw_ref, wg_ref, p_ref, wp_ref, fn_ref, o_ref, *, final):
    x = h_ref[...]
    gate = _sigmoid(_dot(_bf(_rms_norm(x, nw_ref[...])), wg_ref[...]))
    y = x + gate * _dot(_bf(p_ref[...]), wp_ref[...])
    if final:
        y = _rms_norm(y, fn_ref[...])
    o_ref[...] = y


def _ple(h, nw, wg, p, wp, fn, *, layer, final, tm=512):
    t, d = h.shape
    tm = min(tm, t)
    return pl.pallas_call(
        functools.partial(_ple_kernel, final=final),
        grid=(t // tm,),
        in_specs=[
            pl.BlockSpec((tm, d), lambda i: (i, 0)),
            pl.BlockSpec((None, 1, d), lambda i: (layer, 0, 0)),
            pl.BlockSpec((None, d, d), lambda i: (layer, 0, 0)),
            pl.BlockSpec((None, tm, PLE_DIM), lambda i: (layer, i, 0)),
            pl.BlockSpec((None, PLE_DIM, d), lambda i: (layer, 0, 0)),
            pl.BlockSpec((1, d), lambda i: (0, 0)),
        ],
        out_specs=pl.BlockSpec((tm, d), lambda i: (i, 0)),
        out_shape=jax.ShapeDtypeStruct((t, d), F32),
        compiler_params=_params("parallel"),
        name="ple",
    )(h, nw, wg, p, wp, fn)


def _rest_columns(w):
    c0 = QKV_WIDTH
    c1 = c0 + DN_WIDTH
    c2 = c1 + 2 * DN_HEADS
    pad = jnp.zeros(w.shape[:2] + (REST_WIDTH - REST_MAIN - 2 * DN_HEADS,), w.dtype)
    return jnp.concatenate([w[..., c2:], w[..., c0:c1], w[..., c1:c2], pad], axis=-1)


def _lane_rows(x, offset):
    n = x.shape[1]
    return jnp.pad(x.astype(F32), ((0, 0), (offset, LANES - offset - n)))[:, None, :]


def kernel(x, p, ffn1_norm, ffn1_w_gate_up, ffn1_w_down, mix_norm, w_in, dn_conv, dn_a_log, dn_dt_bias, dn_out_norm, hg_lower_bounds, hg_out_norm, rt_out_norm, w_out, ffn2_norm, ffn2_w_gate_up, ffn2_w_down, ple_norm, ple_w_gate, ple_w_proj, final_norm):
    batch, seq, d = x.shape
    t = batch * seq
    h = x.reshape(t, d)
    blk = min(MIXER_BLOCK, seq)
    tables = _tables(seq, blk)
    rows = lambda v: v.astype(F32)[:, None, :]
    ffn1_wgu, ffn1_wd = _bf(ffn1_w_gate_up), _bf(ffn1_w_down)
    ffn2_wgu, ffn2_wd = _bf(ffn2_w_gate_up), _bf(ffn2_w_down)
    w_qkv = _bf(w_in[..., :QKV_WIDTH])
    w_rest = _bf(_rest_columns(w_in))
    w_out_bf, ple_wg, ple_wp = _bf(w_out), _bf(ple_w_gate), _bf(ple_w_proj)
    ffn1_nw, ffn2_nw, mix_nw, ple_nw = rows(ffn1_norm), rows(ffn2_norm), rows(mix_norm), rows(ple_norm)
    dn_gain, hg_gain, rt_gain = rows(dn_out_norm), rows(hg_out_norm), rows(rt_out_norm)
    alog_rows, dtb_rows = _lane_rows(dn_a_log, DN_HEADS), _lane_rows(dn_dt_bias, DN_HEADS)
    p_tok = p.reshape(DEPTH, t, PLE_DIM)
    final_nw = final_norm.astype(F32)[None, :]
    for i in range(DEPTH):
        h = _ffn(h, ffn1_nw, ffn1_wgu, ffn1_wd, layer=i)
        qkv = _qkvproj(h, mix_nw, w_qkv, dn_conv, layer=i, seq=seq)
        rest = _restproj(h, mix_nw, w_rest, layer=i)
        mixed = _mixer(qkv, rest, alog_rows, dtb_rows, dn_gain, hg_lower_bounds, hg_gain, tables, rt_gain,
                       layer=i, batch=batch, seq=seq, blk=blk)
        h = _outproj(h, mixed, w_out_bf, layer=i)
        h = _ffn(h, ffn2_nw, ffn2_wgu, ffn2_wd, layer=i)
        h = _ple(h, ple_nw, ple_wg, p_tok, ple_wp, final_nw, layer=i, final=(i == DEPTH - 1))
    return h.reshape(batch, seq, d)
```
